```python
import math
import jax
import jax.numpy as jnp
from jax import lax
import numpy as np

D_MODEL = 1024
BATCH = 32
SEQ = 2048
DEPTH = 4

GRID_W = 64
CTX_LEN = 256
HEAD_DIM = 64
ROPE_THETA = 10000.0
BLOCK = 128
BRANCH_W = D_MODEL // 2
N_BRANCH = 4
WA_HEADS = BRANCH_W // HEAD_DIM
WA_KV_HEADS = 2
WA_GROUP = WA_HEADS // WA_KV_HEADS
WINDOW = 128
DIFF_HEADS = BRANCH_W // (2 * HEAD_DIM)
CONV_CH = BRANCH_W
CONV_K = 31
POOL_CH = BRANCH_W
POOL_WINDOWS = (2, 4, 8, 16)
POOL_GROUPS = len(POOL_WINDOWS)
POOL_GC = POOL_CH // POOL_GROUPS
PEER_HEADS = 8
PEER_NKEYS = 128
PEER_EXPERTS = PEER_NKEYS * PEER_NKEYS
PEER_DKEY = 256
PEER_TOPK = 16
PEER_CHUNK = 128
PEER_V_SCALE = 0.5
EPS = 1e-6
NEG_INF = -1e30

SEG_NAMES = ('a_q', 'a_k', 'a_v', 'c_q', 'c_k', 'c_v', 'b_in', 'd_in')
SEG_SIZES = (WA_HEADS * HEAD_DIM, WA_KV_HEADS * HEAD_DIM, WA_KV_HEADS * HEAD_DIM,
             2 * DIFF_HEADS * HEAD_DIM, 2 * DIFF_HEADS * HEAD_DIM, 2 * DIFF_HEADS * HEAD_DIM,
             2 * CONV_CH, POOL_CH)
SEG_START = tuple(int(s) for s in np.cumsum((0,) + SEG_SIZES[:-1]))
IN_COLS = int(sum(SEG_SIZES))
CTX_KV_SEGS = ('a_k', 'a_v', 'c_k', 'c_v')

kernel_name = 'hybrid_parallel_gated_peer_dit'


def rms_norm(x, g):
    xf = x.astype(jnp.float32)
    y = xf * lax.rsqrt(jnp.mean(xf * xf, axis=-1, keepdims=True) + EPS)
    return (y * g.astype(jnp.float32)).astype(x.dtype)


def layer_norm(x, g, b):
    xf = x.astype(jnp.float32)
    mu = jnp.mean(xf, axis=-1, keepdims=True)
    var = jnp.mean(jnp.square(xf - mu), axis=-1, keepdims=True)
    y = (xf - mu) * lax.rsqrt(var + EPS)
    return (y * g.astype(jnp.float32) + b.astype(jnp.float32)).astype(x.dtype)


def ada_modulation(cvec, w, b):
    return jnp.split(jax.nn.silu(cvec) @ w + b, 6, axis=-1)


def modulate(x, g, shift, scale):
    return rms_norm(x, g) * (1.0 + scale) + shift


def axial_rope(length):
    rows = length // GRID_W
    row = jnp.repeat(jnp.arange(rows, dtype=jnp.float32), GRID_W)
    col = jnp.tile(jnp.arange(GRID_W, dtype=jnp.float32), rows)
    n_freq = HEAD_DIM // 4
    inv_freq = ROPE_THETA ** (-jnp.arange(n_freq, dtype=jnp.float32) / n_freq)
    ang = jnp.concatenate([row[:, None] * inv_freq, col[:, None] * inv_freq], axis=-1)
    return jnp.cos(ang), jnp.sin(ang)


def apply_rope(x, cos, sin):
    half = x.shape[-1] // 2
    shape = (x.shape[1],) + (1,) * (x.ndim - 3) + (half,)
    cos = cos.reshape(shape).astype(x.dtype)
    sin = sin.reshape(shape).astype(x.dtype)
    x1, x2 = x[..., :half], x[..., half:]
    return jnp.concatenate([x1 * cos - x2 * sin, x2 * cos + x1 * sin], axis=-1)


def in_proj(h, w_in, names):
    idx = [SEG_NAMES.index(n) for n in names]
    w = jnp.concatenate([w_in[:, SEG_START[j]:SEG_START[j] + SEG_SIZES[j]] for j in idx], axis=1)
    splits = [int(s) for s in np.cumsum([SEG_SIZES[j] for j in idx])[:-1]]
    return dict(zip(names, jnp.split(h @ w, splits, axis=-1)))


def gqa_sink_attend(q, k, v, sink, valid):
    s = jnp.einsum('bqkgd,bjkd->bkgqj', q, k).astype(jnp.float32) * (HEAD_DIM ** -0.5)
    if valid is not None:
        s = jnp.where(valid, s, NEG_INF)
    hkv, grp = q.shape[2], q.shape[3]
    sink_col = jnp.broadcast_to(sink.astype(jnp.float32).reshape(hkv, grp, 1, 1), s.shape[:-1] + (1,))
    p = jax.nn.softmax(jnp.concatenate([s, sink_col], axis=-1), axis=-1)[..., :-1]
    return jnp.einsum('bkgqj,bjkd->bqkgd', p.astype(v.dtype), v)


def window_attention(q, k, v, ck, cv, sink):
    B, L = q.shape[:2]
    Lc = ck.shape[1]
    nb = L // BLOCK
    pad = ((0, 0), (BLOCK, BLOCK), (0, 0), (0, 0))
    kp, vp = jnp.pad(k, pad), jnp.pad(v, pad)
    qblocks = jnp.moveaxis(q.reshape((B, nb, BLOCK) + q.shape[2:]), 1, 0)
    offs = jnp.arange(3 * BLOCK) - BLOCK
    qi = jnp.arange(BLOCK)
    ctx_ok = jnp.ones((BLOCK, Lc), dtype=bool)

    def one_block(args):
        n, qb = args
        start = n * BLOCK
        kb = lax.dynamic_slice_in_dim(kp, start, 3 * BLOCK, axis=1)
        vb = lax.dynamic_slice_in_dim(vp, start, 3 * BLOCK, axis=1)
        kabs = start + offs
        band = ((jnp.abs(kabs[None, :] - (start + qi)[:, None]) <= WINDOW)
                & (kabs >= 0)[None, :] & (kabs < L)[None, :])
        valid = jnp.concatenate([band, ctx_ok], axis=1)
        return gqa_sink_attend(qb, jnp.concatenate([kb, ck], axis=1),
                               jnp.concatenate([vb, cv], axis=1), sink, valid)

    o = lax.map(one_block, (jnp.arange(nb), qblocks))
    return jnp.moveaxis(o, 0, 1).reshape(B, L, WA_HEADS * HEAD_DIM)


def diff_core(q, k, v, lam):
    s = jnp.einsum('bqhcd,bjhcd->bhcqj', q, k).astype(jnp.float32) * (HEAD_DIM ** -0.5)
    p = jax.nn.softmax(s, axis=-1)
    a = p[:, :, 0] - lam * p[:, :, 1]
    return jnp.einsum('bhqj,bjhe->bqhe', a.astype(v.dtype), v)


def dense_diff_attention(q, k, v, ck, cv, lam):
    B, L = q.shape[:2]
    nb = L // BLOCK
    k_all = jnp.concatenate([k, ck], axis=1)
    v_all = jnp.concatenate([v, cv], axis=1)
    qblocks = jnp.moveaxis(q.reshape((B, nb, BLOCK) + q.shape[2:]), 1, 0)
    o = lax.map(lambda qb: diff_core(qb, k_all, v_all, lam), qblocks)
    return jnp.moveaxis(o, 0, 1).reshape(B, L, DIFF_HEADS, 2 * HEAD_DIM)


def diff_output(o, g, lam_init):
    B, L = o.shape[:2]
    return (rms_norm(o, g) * (1.0 - lam_init)).reshape(B, L, DIFF_HEADS * 2 * HEAD_DIM)


def conformer_conv(u, w, b, ln_g, ln_b):
    a, gt = jnp.split(u, 2, axis=-1)
    y = a * jax.nn.sigmoid(gt)
    y = lax.conv_general_dilated(y, w[:, None, :], window_strides=(1,),
                                 padding=((CONV_K // 2, CONV_K // 2),),
                                 dimension_numbers=('NWC', 'WIO', 'NWC'),
                                 feature_group_count=CONV_CH) + b
    return jax.nn.silu(layer_norm(y, ln_g, ln_b))


def pool_mixer(u, w_grp, scale):
    B, L, _ = u.shape
    ug = u.reshape(B, L, POOL_GROUPS, POOL_GC)
    t = jnp.arange(L)
    outs = []
    for g, win in enumerate(POOL_WINDOWS):
        lo, hi = win // 2, win - 1 - win // 2
        xg = ug[:, :, g].astype(jnp.float32)
        cs = lax.cumsum(jnp.pad(xg, ((0, 0), (1 + lo, hi), (0, 0))), axis=1)
        win_sum = cs[:, win:] - cs[:, :L]
        cnt = (jnp.minimum(t + hi, L - 1) - jnp.maximum(t - lo, 0) + 1).astype(jnp.float32)
        outs.append(win_sum / cnt[:, None] - xg)
    y = jnp.stack(outs, axis=2).astype(u.dtype)
    y = jnp.einsum('blgc,gce->blge', y, w_grp)
    return y.reshape(B, L, POOL_CH) * scale


def merge_branches(h, ys, w_gate, b_gate, w_branch, w_out):
    acc = jax.nn.sigmoid(h @ w_gate[0] + b_gate[0]) * (ys[0] @ w_branch[0])
    for n in range(1, N_BRANCH):
        acc = acc + jax.nn.sigmoid(h @ w_gate[n] + b_gate[n]) * (ys[n] @ w_branch[n])
    return acc @ w_out


def token_mixer(h, hc, w_in, w_gate, b_gate, sink, lam, lam_init, diff_g, conv_w, conv_b,
                ln_g, ln_b, pool_w, pool_scale, w_branch, w_out, ctx_out):
    B, L, _ = h.shape
    Lc = hc.shape[1]
    cos, sin = axial_rope(L)
    s = in_proj(h, w_in, SEG_NAMES)
    sc = in_proj(hc, w_in, SEG_NAMES if ctx_out else CTX_KV_SEGS)
    cka = sc['a_k'].reshape(B, Lc, WA_KV_HEADS, HEAD_DIM)
    cva = sc['a_v'].reshape(B, Lc, WA_KV_HEADS, HEAD_DIM)
    ckc = sc['c_k'].reshape(B, Lc, DIFF_HEADS, 2, HEAD_DIM)
    cvc = sc['c_v'].reshape(B, Lc, DIFF_HEADS, 2 * HEAD_DIM)
    qa = apply_rope(s['a_q'].reshape(B, L, WA_KV_HEADS, WA_GROUP, HEAD_DIM), cos, sin)
    ka = apply_rope(s['a_k'].reshape(B, L, WA_KV_HEADS, HEAD_DIM), cos, sin)
    va = s['a_v'].reshape(B, L, WA_KV_HEADS, HEAD_DIM)
    y_a = window_attention(qa, ka, va, cka, cva, sink)
    qc = apply_rope(s['c_q'].reshape(B, L, DIFF_HEADS, 2, HEAD_DIM), cos, sin)
    kc = apply_rope(s['c_k'].reshape(B, L, DIFF_HEADS, 2, HEAD_DIM), cos, sin)
    vc = s['c_v'].reshape(B, L, DIFF_HEADS, 2 * HEAD_DIM)
    y_c = diff_output(dense_diff_attention(qc, kc, vc, ckc, cvc, lam), diff_g, lam_init)
    y_b = conformer_conv(s['b_in'], conv_w, conv_b, ln_g, ln_b)
    y_d = pool_mixer(s['d_in'], pool_w, pool_scale)
    out = merge_branches(h, (y_a, y_c, y_b, y_d), w_gate, b_gate, w_branch, w_out)
    if not ctx_out:
        return out, None
    cqa = sc['a_q'].reshape(B, Lc, WA_KV_HEADS, WA_GROUP, HEAD_DIM)
    yc_a = gqa_sink_attend(cqa, cka, cva, sink, None).reshape(B, Lc, WA_HEADS * HEAD_DIM)
    cqc = sc['c_q'].reshape(B, Lc, DIFF_HEADS, 2, HEAD_DIM)
    yc_c = diff_output(diff_core(cqc, ckc, cvc, lam), diff_g, lam_init)
    yc_b = conformer_conv(sc['b_in'], conv_w, conv_b, ln_g, ln_b)
    yc_d = pool_mixer(sc['d_in'], pool_w, pool_scale)
    out_c = merge_branches(hc, (yc_a, yc_c, yc_b, yc_d), w_gate, b_gate, w_branch, w_out)
    return out, out_c


def peer_ffn(h, wq, k1, k2, u_tab, v_tab):
    B, L, D = h.shape
    hc = h.reshape(B * L // PEER_CHUNK, PEER_CHUNK, D)

    def chunk(xc):
        q = (xc @ wq).reshape(PEER_CHUNK, PEER_HEADS, 2, PEER_DKEY // 2)
        s1 = jnp.einsum('thd,hnd->thn', q[:, :, 0], k1)
        s2 = jnp.einsum('thd,hnd->thn', q[:, :, 1], k2)
        v1, i1 = lax.top_k(s1, PEER_TOPK)
        v2, i2 = lax.top_k(s2, PEER_TOPK)
        cand = (v1[..., :, None] + v2[..., None, :]).reshape(PEER_CHUNK, PEER_HEADS, PEER_TOPK * PEER_TOPK)
        sc, ci = lax.top_k(cand, PEER_TOPK)
        idx = (jnp.take_along_axis(i1, ci // PEER_TOPK, axis=-1) * PEER_NKEYS
               + jnp.take_along_axis(i2, ci % PEER_TOPK, axis=-1))
        gate = jax.nn.softmax(sc.astype(jnp.float32), axis=-1).astype(xc.dtype)
        u = jnp.take(u_tab, idx, axis=0)
        act = jax.nn.gelu(jnp.einsum('td,thkd->thk', xc, u), approximate=False)
        v = jnp.take(v_tab, idx, axis=0)
        return jnp.einsum('thk,thkd->td', gate * act, v)

    return lax.map(chunk, hc).reshape(B, L, D)


def setup_inputs(seed: int = 0) -> dict:
    key = jax.random.key(seed)
    ks = jax.random.split(key, 40)
    D = D_MODEL

    def nrm(k, shape, scale):
        return jax.random.normal(k, shape, dtype=jnp.float32) * scale

    return {
        'x': nrm(ks[0], (BATCH, SEQ, D), 1.0),
        'c': nrm(ks[1], (BATCH, D), 1.0),
        'ctx': nrm(ks[2], (BATCH, CTX_LEN, D), 1.0),
        'c_ctx': nrm(ks[3], (D,), 1.0),
        'w_mod': nrm(ks[4], (DEPTH, D, 6 * D), 0.5 * D ** -0.5),
        'b_mod': nrm(ks[5], (DEPTH, 6 * D), 0.02),
        'norm1_g': 1.0 + nrm(ks[6], (DEPTH, D), 0.02),
        'norm2_g': 1.0 + nrm(ks[7], (DEPTH, D), 0.02),
        'w_in': nrm(ks[8], (DEPTH, D, IN_COLS), D ** -0.5),
        'w_gate': nrm(ks[9], (DEPTH, N_BRANCH, D, D), D ** -0.5),
        'b_gate': nrm(ks[10], (DEPTH, N_BRANCH, D), 0.02),
        'attn_sink': nrm(ks[11], (DEPTH, WA_HEADS), 0.5),
        'lam_q1': nrm(ks[12], (DEPTH, HEAD_DIM), 0.1),
        'lam_k1': nrm(ks[13], (DEPTH, HEAD_DIM), 0.1),
        'lam_q2': nrm(ks[14], (DEPTH, HEAD_DIM), 0.1),
        'lam_k2': nrm(ks[15], (DEPTH, HEAD_DIM), 0.1),
        'diff_norm_g': 1.0 + nrm(ks[16], (DEPTH, 2 * HEAD_DIM), 0.02),
        'conv_w': nrm(ks[17], (DEPTH, CONV_K, CONV_CH), CONV_K ** -0.5),
        'conv_b': nrm(ks[18], (DEPTH, CONV_CH), 0.02),
        'conv_ln_g': 1.0 + nrm(ks[19], (DEPTH, CONV_CH), 0.02),
        'conv_ln_b': nrm(ks[20], (DEPTH, CONV_CH), 0.02),
        'pool_w': nrm(ks[21], (DEPTH, POOL_GROUPS, POOL_GC, POOL_GC), POOL_GC ** -0.5),
        'pool_scale': 1.0 + nrm(ks[22], (DEPTH, POOL_CH), 0.1),
        'w_branch': nrm(ks[23], (DEPTH, N_BRANCH, BRANCH_W, D), BRANCH_W ** -0.5),
        'w_out': nrm(ks[24], (DEPTH, D, D), D ** -0.5),
        'peer_wq': nrm(ks[25], (DEPTH, D, PEER_HEADS * PEER_DKEY), D ** -0.5),
        'peer_k1': nrm(ks[26], (DEPTH, PEER_HEADS, PEER_NKEYS, PEER_DKEY // 2), (PEER_DKEY // 2) ** -0.5),
        'peer_k2': nrm(ks[27], (DEPTH, PEER_HEADS, PEER_NKEYS, PEER_DKEY // 2), (PEER_DKEY // 2) ** -0.5),
        'peer_u': nrm(ks[28], (DEPTH, PEER_EXPERTS, D), D ** -0.5),
        'peer_v': nrm(ks[29], (DEPTH, PEER_EXPERTS, D), PEER_V_SCALE),
        'final_g': 1.0 + nrm(ks[30], (D,), 0.02),
    }


def reference(x, c, ctx, c_ctx, w_mod, b_mod, norm1_g, norm2_g, w_in, w_gate, b_gate, attn_sink,
              lam_q1, lam_k1, lam_q2, lam_k2, diff_norm_g, conv_w, conv_b, conv_ln_g, conv_ln_b,
              pool_w, pool_scale, w_branch, w_out, peer_wq, peer_k1, peer_k2, peer_u, peer_v, final_g):
    xc = ctx
    c_lat = c[:, None, :]
    c_con = c_ctx[None, None, :]
    for i in range(DEPTH):
        ctx_out = i < DEPTH - 1
        lam_init = 0.8 - 0.6 * math.exp(-0.3 * i)
        lam = (jnp.exp(jnp.sum(lam_q1[i].astype(jnp.float32) * lam_k1[i].astype(jnp.float32)))
               - jnp.exp(jnp.sum(lam_q2[i].astype(jnp.float32) * lam_k2[i].astype(jnp.float32)))
               + lam_init)
        sh1, sc1, g1, sh2, sc2, g2 = ada_modulation(c_lat, w_mod[i], b_mod[i])
        csh1, csc1, cg1, csh2, csc2, cg2 = ada_modulation(c_con, w_mod[i], b_mod[i])
        h = modulate(x, norm1_g[i], sh1, sc1)
        hc = modulate(xc, norm1_g[i], csh1, csc1)
        y, yc = token_mixer(h, hc, w_in[i], w_gate[i], b_gate[i], attn_sink[i], lam, lam_init,
                            diff_norm_g[i], conv_w[i], conv_b[i], conv_ln_g[i], conv_ln_b[i],
                            pool_w[i], pool_scale[i], w_branch[i], w_out[i], ctx_out)
        x = x + g1 * y
        h2 = modulate(x, norm2_g[i], sh2, sc2)
        x = x + g2 * peer_ffn(h2, peer_wq[i], peer_k1[i], peer_k2[i], peer_u[i], peer_v[i])
        if ctx_out:
            xc = xc + cg1 * yc
            hc2 = modulate(xc, norm2_g[i], csh2, csc2)
            xc = xc + cg2 * peer_ffn(hc2, peer_wq[i], peer_k1[i], peer_k2[i], peer_u[i], peer_v[i])
    return rms_norm(x, final_g)
```

```python
import functools
import math

import jax
import jax.numpy as jnp
from jax import lax
from jax.experimental import pallas as pl
from jax.experimental.pallas import tpu as pltpu
from jax.experimental.pallas import tpu_sc as plsc

GRID_W = 64
HEAD_DIM = 64
ROPE_THETA = 10000.0
BLOCK = 128
WINDOW = 128
N_BRANCH = 4
WA_HEADS = 8
WA_KV_HEADS = 2
WA_GROUP = WA_HEADS // WA_KV_HEADS
DIFF_HEADS = 4
CONV_K = 31
POOL_WINDOWS = (2, 4, 8, 16)
PEER_HEADS = 8
PEER_NKEYS = 128
PEER_TOPK = 16
EPS = 1e-6
NEG_INF = -1e30
NEG_BIG = -3.0e38
BRANCH_W = WA_HEADS * HEAD_DIM
A_COLS = BRANCH_W + 2 * WA_KV_HEADS * HEAD_DIM
C_COLS = 3 * BRANCH_W
BD_COLS = 3 * BRANCH_W

LANES = 128
VMEM_LIMIT = 48 * 1024 * 1024
SC_LANES = 16
SC_WORKERS = 32
PEER_ROWS = PEER_HEADS * PEER_TOPK
GATHER_ROWS = 32
SC_TOKEN_BLOCK = 8

F32 = jnp.float32
BF16 = jnp.bfloat16


def _cparams(*sem):
    return pltpu.CompilerParams(dimension_semantics=sem, vmem_limit_bytes=VMEM_LIMIT)


def _pick_tile(candidates, *sizes):
    for t in candidates:
        if all(s % t == 0 for s in sizes):
            return t
    raise ValueError(f"no tile in {candidates} divides {sizes}")


def _rms(x):
    return x * lax.rsqrt(jnp.mean(x * x, axis=-1, keepdims=True) + EPS)


def _sc_mesh():
    return plsc.VectorSubcoreMesh(core_axis_name="c", subcore_axis_name="s")


def _sc_worker_id():
    return lax.axis_index("s") * 2 + lax.axis_index("c")


def _lane_broadcast(vec, lane, r):
    return jnp.full((SC_LANES,), jnp.sum(jnp.where(lane == r, vec, 0.0)), F32)


def _sc_peer_call(kind, a, idx, table, n_tokens):
    T = n_tokens
    D = table.shape[1]
    TB, GR = SC_TOKEN_BLOCK, GATHER_ROWS
    assert T % (SC_WORKERS * TB) == 0 and D % SC_LANES == 0 and idx.shape[1] == PEER_ROWS
    tpw = T // SC_WORKERS
    n_chunks = D // SC_LANES
    n_groups = PEER_ROWS // GR
    a_width = a.shape[1]
    out_width = PEER_ROWS if kind == "dots" else D

    def body(a_hbm, idx_hbm, tab_hbm, out_hbm, idx_v, a_v, rows0, rows1, out_v, sem0, sem1):
        wid = _sc_worker_id()
        lane = lax.iota(jnp.int32, SC_LANES)
        rows = (rows0, rows1)
        sems = (sem0, sem1)

        def gather(t, g):
            return pltpu.make_async_copy(tab_hbm.at[idx_v.at[t, pl.ds(g * GR, GR)]], rows[g % 2], sems[g % 2])

        def dots_group(t, g):
            rbuf = rows[g % 2]
            for half in range(GR // SC_LANES):
                r0 = half * SC_LANES

                def chunk(c, accs):
                    xc = a_v[t, pl.ds(c * SC_LANES, SC_LANES)]
                    return tuple(acc + xc * rbuf[r0 + r, pl.ds(c * SC_LANES, SC_LANES)]
                                 for r, acc in enumerate(accs))

                accs = lax.fori_loop(0, n_chunks, chunk,
                                     tuple(jnp.zeros((SC_LANES,), F32) for _ in range(SC_LANES)))
                tot = jnp.zeros((SC_LANES,), F32)
                for r in range(SC_LANES):
                    tot = jnp.where(lane == r, jnp.sum(accs[r]), tot)
                out_v[t, pl.ds(g * GR + r0, SC_LANES)] = tot

        def wsum_group(t, g):
            rbuf = rows[g % 2]
            for half in range(GR // SC_LANES):
                r0 = half * SC_LANES
                wvec = a_v[t, pl.ds(g * GR + r0, SC_LANES)]
                wb = [_lane_broadcast(wvec, lane, r) for r in range(SC_LANES)]
                first = (g == 0 and half == 0)

                @pl.loop(0, n_chunks)
                def _(c):
                    sl = pl.ds(c * SC_LANES, SC_LANES)
                    acc = jnp.zeros((SC_LANES,), F32) if first else out_v[t, sl]
                    for r in range(SC_LANES):
                        acc = acc + wb[r] * rbuf[r0 + r, sl]
                    out_v[t, sl] = acc

        group = dots_group if kind == "dots" else wsum_group

        @pl.loop(0, tpw // TB)
        def _(b):
            tok0 = wid * tpw + b * TB
            pltpu.sync_copy(idx_hbm.at[pl.ds(tok0, TB)], idx_v)
            pltpu.sync_copy(a_hbm.at[pl.ds(tok0, TB)], a_v)
            gather(0, 0).start()

            @pl.loop(0, TB)
            def _(t):
                for g in range(n_groups):
                    if g + 1 < n_groups:
                        gather(t, g + 1).start()
                    else:
                        @pl.when(t + 1 < TB)
                        def _():
                            gather(t + 1, 0).start()
                    gather(t, g).wait()
                    group(t, g)

            pltpu.sync_copy(out_v, out_hbm.at[pl.ds(tok0, TB)])

    f = pl.kernel(
        body,
        out_type=jax.ShapeDtypeStruct((T, out_width), F32),
        mesh=_sc_mesh(),
        compiler_params=pltpu.CompilerParams(needs_layout_passes=False),
        scratch_types=[
            pltpu.VMEM((TB, PEER_ROWS), jnp.int32),
            pltpu.VMEM((TB, a_width), F32),
            pltpu.VMEM((GR, D), F32),
            pltpu.VMEM((GR, D), F32),
            pltpu.VMEM((TB, out_width), F32),
            pltpu.SemaphoreType.DMA,
            pltpu.SemaphoreType.DMA,
        ],
    )
    return f(a, idx, table)


def sc_peer_dots(x, idx, table, n_tokens):
    return _sc_peer_call("dots", x, idx, table, n_tokens)


def sc_peer_wsum(w, idx, table, n_tokens):
    return _sc_peer_call("wsum", w, idx, table, n_tokens)


def _mod_kernel(c_ref, w_ref, b_ref, o_ref):
    cv = c_ref[...]
    s = cv * jax.nn.sigmoid(cv)
    o_ref[...] = jnp.dot(s, w_ref[...], precision=lax.Precision.HIGHEST,
                         preferred_element_type=F32) + b_ref[...]


def mod_vectors(cvec, w_mod, b_mod):
    R, D = cvec.shape
    W = w_mod.shape[1]
    tn = _pick_tile((512, 256, 128), W)
    return pl.pallas_call(
        _mod_kernel,
        grid=(W // tn,),
        in_specs=[pl.BlockSpec((R, D), lambda j: (0, 0)),
                  pl.BlockSpec((D, tn), lambda j: (0, j)),
                  pl.BlockSpec((1, tn), lambda j: (0, j))],
        out_specs=pl.BlockSpec((R, tn), lambda j: (0, j)),
        out_shape=jax.ShapeDtypeStruct((R, W), F32),
        compiler_params=_cparams("arbitrary"),
        name="mod_vectors",
    )(cvec, w_mod, b_mod.reshape(1, W))


def _pre_kernel(has_ffn, final, *refs):
    if has_ffn:
        x_ref, ffn_ref, modp_ref, mod_ref, g_ref = refs[:5]
        outs = refs[5:]
        x = x_ref[...] + modp_ref[0, 5:6, :] * ffn_ref[...]
    else:
        x_ref, mod_ref, g_ref = refs[:3]
        outs = refs[3:]
        x = x_ref[...]
    y = _rms(x) * g_ref[...]
    if final:
        outs[0][...] = y
        return
    h = y * (1.0 + mod_ref[0, 1:2, :]) + mod_ref[0, 0:1, :]
    if has_ffn:
        outs[0][...] = x
        outs[1][...] = h.astype(BF16)
    else:
        outs[0][...] = h.astype(BF16)


def pre_norm(geo, n_tokens, x, ffn, mod_prev, mod, gain, final=False):
    D = x.shape[1]
    tm = geo["tm_tok"]
    grp = geo["group_map"](tm)
    has_ffn = ffn is not None
    row = pl.BlockSpec((tm, D), lambda i: (i, 0))
    modspec = pl.BlockSpec((1, 6, D), lambda i: (grp(i), 0, 0))
    gspec = pl.BlockSpec((1, D), lambda i: (0, 0))
    if has_ffn:
        args = (x, ffn, mod_prev, mod, gain)
        in_specs = [row, row, modspec, modspec, gspec]
    else:
        args = (x, mod, gain)
        in_specs = [row, modspec, gspec]
    n_all = x.shape[0]
    if final:
        out_shape = [jax.ShapeDtypeStruct((n_tokens, D), F32)]
    elif has_ffn:
        out_shape = [jax.ShapeDtypeStruct((n_all, D), F32), jax.ShapeDtypeStruct((n_all, D), BF16)]
    else:
        out_shape = [jax.ShapeDtypeStruct((n_all, D), BF16)]
    outs = pl.pallas_call(
        functools.partial(_pre_kernel, has_ffn, final),
        grid=(n_tokens // tm,),
        in_specs=in_specs,
        out_specs=[row] * len(out_shape),
        out_shape=out_shape,
        compiler_params=_cparams("parallel"),
        name="pre_norm",
    )(*args)
    return outs


def _mm_kernel(epi, tn, a_ref, w_ref, *rest):
    acc = jnp.dot(a_ref[...], w_ref[...], preferred_element_type=F32)
    if epi == "rope":
        cos_ref, sin_ref, o_ref = rest
        for ch in range(tn // LANES):
            sl = slice(ch * LANES, (ch + 1) * LANES)
            a = acc[:, sl]
            lane = lax.broadcasted_iota(jnp.int32, a.shape, 1)
            partner = jnp.where((lane % HEAD_DIM) < HEAD_DIM // 2,
                                pltpu.roll(a, LANES - HEAD_DIM // 2, 1), pltpu.roll(a, HEAD_DIM // 2, 1))
            o_ref[:, sl] = (a * cos_ref[:, sl] + partner * sin_ref[:, sl]).astype(o_ref.dtype)
    elif epi == "sigmoid":
        b_ref, o_ref = rest
        o_ref[...] = jax.nn.sigmoid(acc + b_ref[...]).astype(o_ref.dtype)
    else:
        (o_ref,) = rest
        o_ref[...] = acc.astype(o_ref.dtype)


def token_matmul(geo, n_tokens, a, w, epi="none", extras=()):
    n_all, K = a.shape
    W = w.shape[1]
    tm = geo["tm_mm"]
    tn = _pick_tile((768, 512, 256, 128), W)
    in_specs = [pl.BlockSpec((tm, K), lambda i, j: (i, 0)), pl.BlockSpec((K, tn), lambda i, j: (0, j))]
    if epi == "rope":
        pos = geo["pos_map"](tm)
        in_specs += [pl.BlockSpec((tm, tn), lambda i, j: (pos(i), j))] * 2
    elif epi == "sigmoid":
        in_specs += [pl.BlockSpec((1, tn), lambda i, j: (0, j))]
    return pl.pallas_call(
        functools.partial(_mm_kernel, epi, tn),
        grid=(n_tokens // tm, W // tn),
        in_specs=in_specs,
        out_specs=pl.BlockSpec((tm, tn), lambda i, j: (i, j)),
        out_shape=jax.ShapeDtypeStruct((n_all, W), BF16),
        compiler_params=_cparams("parallel", "arbitrary"),
        name="token_matmul_" + epi,
    )(a, w, *extras)


def _attn_a_kernel(nb, seq_len, sink_ref, cur_ref, prev_ref, next_ref, ctx_ref, o_ref):
    n = pl.program_id(1)
    is_lat = n < nb
    qi = lax.broadcasted_iota(jnp.int32, (BLOCK, 3 * BLOCK), 0)
    kj = lax.broadcasted_iota(jnp.int32, (BLOCK, 3 * BLOCK), 1)
    kabs = (n - 1) * BLOCK + kj
    band = (jnp.abs(kj - BLOCK - qi) <= WINDOW) & (kabs >= 0) & (kabs < seq_len) & is_lat
    nt = (((1,), (1,)), ((), ()))
    kv_off = WA_HEADS * HEAD_DIM
    for kvh in range(WA_KV_HEADS):
        kc = slice(kv_off + kvh * HEAD_DIM, kv_off + (kvh + 1) * HEAD_DIM)
        vc = slice(kv_off + (WA_KV_HEADS + kvh) * HEAD_DIM, kv_off + (WA_KV_HEADS + kvh + 1) * HEAD_DIM)
        k_lat = jnp.concatenate([prev_ref[:, kc], cur_ref[:, kc], next_ref[:, kc]], axis=0)
        v_lat = jnp.concatenate([prev_ref[:, vc], cur_ref[:, vc], next_ref[:, vc]], axis=0)
        k_ctx = ctx_ref[:, kc]
        v_ctx = ctx_ref[:, vc]
        for g in range(WA_GROUP):
            h = kvh * WA_GROUP + g
            qh = cur_ref[:, h * HEAD_DIM:(h + 1) * HEAD_DIM]
            s1 = lax.dot_general(qh, k_lat, nt, preferred_element_type=F32) * (HEAD_DIM ** -0.5)
            s1 = jnp.where(band, s1, NEG_INF)
            s2 = lax.dot_general(qh, k_ctx, nt, preferred_element_type=F32) * (HEAD_DIM ** -0.5)
            sink = sink_ref[h]
            m = jnp.maximum(jnp.maximum(jnp.max(s1, axis=1, keepdims=True),
                                        jnp.max(s2, axis=1, keepdims=True)), sink)
            e1 = jnp.exp(s1 - m)
            e2 = jnp.exp(s2 - m)
            den = jnp.sum(e1, axis=1, keepdims=True) + jnp.sum(e2, axis=1, keepdims=True) + jnp.exp(sink - m)
            pv = (jnp.dot(e1.astype(BF16), v_lat, preferred_element_type=F32)
                  + jnp.dot(e2.astype(BF16), v_ctx, preferred_element_type=F32))
            o_ref[:, h * HEAD_DIM:(h + 1) * HEAD_DIM] = (pv / den).astype(o_ref.dtype)


def window_attention(geo, sa, sink, ctx_out):
    B, L, Lc, n_lat, n_all = geo["B"], geo["L"], geo["Lc"], geo["n_lat"], geo["n_all"]
    nb, ncb = L // BLOCK, Lc // BLOCK
    lat0, ctx0 = n_lat // BLOCK, n_lat // Lc

    def cur(b, n):
        return jnp.where(n < nb, b * nb + n, lat0 + b * ncb + (n - nb))

    def prev(b, n):
        return jnp.where(n < nb, b * nb + jnp.maximum(n - 1, 0), lat0 + b * ncb + (n - nb))

    def nxt(b, n):
        return jnp.where(n < nb, b * nb + jnp.minimum(n + 1, nb - 1), lat0 + b * ncb + (n - nb))

    blk = (BLOCK, A_COLS)
    return pl.pallas_call(
        functools.partial(_attn_a_kernel, nb, L),
        grid=(B, nb + (ncb if ctx_out else 0)),
        in_specs=[pl.BlockSpec(memory_space=pltpu.SMEM),
                  pl.BlockSpec(blk, lambda b, n: (cur(b, n), 0)),
                  pl.BlockSpec(blk, lambda b, n: (prev(b, n), 0)),
                  pl.BlockSpec(blk, lambda b, n: (nxt(b, n), 0)),
                  pl.BlockSpec((Lc, A_COLS), lambda b, n: (ctx0 + b, 0))],
        out_specs=pl.BlockSpec((BLOCK, BRANCH_W), lambda b, n: (cur(b, n), 0)),
        out_shape=jax.ShapeDtypeStruct((n_all, BRANCH_W), BF16),
        compiler_params=_cparams("parallel", "arbitrary"),
        name="window_attention",
    )(sink, sa, sa, sa, sa)


def _attn_c_kernel(seq_len, ctx_len, tq, lam_init, lat_ref, ctx_ref, lam_ref, g_ref, o_ref, kall, vall):
    n = pl.program_id(1)
    nlq = seq_len // tq
    kcol = slice(BRANCH_W, 2 * BRANCH_W)
    vcol = slice(2 * BRANCH_W, 3 * BRANCH_W)

    @pl.when(n == 0)
    def _():
        kall[0:seq_len, :] = lat_ref[:, kcol]
        kall[seq_len:seq_len + ctx_len, :] = ctx_ref[:, kcol]
        vall[0:seq_len, :] = lat_ref[:, vcol]
        vall[seq_len:seq_len + ctx_len, :] = ctx_ref[:, vcol]

    lam = (jnp.exp(jnp.sum(lam_ref[0:1, :] * lam_ref[1:2, :], axis=1, keepdims=True))
           - jnp.exp(jnp.sum(lam_ref[2:3, :] * lam_ref[3:4, :], axis=1, keepdims=True)) + lam_init)
    nt = (((1,), (1,)), ((), ()))

    def heads(q, k0, klen):
        for h in range(DIFF_HEADS):
            probs = []
            for c in range(2):
                col = slice(h * 2 * HEAD_DIM + c * HEAD_DIM, h * 2 * HEAD_DIM + (c + 1) * HEAD_DIM)
                s = lax.dot_general(q[:, col], kall[k0:k0 + klen, col], nt,
                                    preferred_element_type=F32) * (HEAD_DIM ** -0.5)
                e = jnp.exp(s - jnp.max(s, axis=1, keepdims=True))
                probs.append(e * (1.0 / jnp.sum(e, axis=1, keepdims=True)))
            a = probs[0] - lam * probs[1]
            hv = slice(h * 2 * HEAD_DIM, (h + 1) * 2 * HEAD_DIM)
            o = jnp.dot(a.astype(BF16), vall[k0:k0 + klen, hv], preferred_element_type=F32)
            o_ref[:, hv] = (_rms(o) * g_ref[...] * (1.0 - lam_init)).astype(o_ref.dtype)

    @pl.when(n < nlq)
    def _():
        heads(lat_ref[pl.ds(pl.multiple_of(n * tq, tq), tq), 0:BRANCH_W], 0, seq_len + ctx_len)

    @pl.when(n >= nlq)
    def _():
        heads(ctx_ref[pl.ds(pl.multiple_of((n - nlq) * tq, tq), tq), 0:BRANCH_W], seq_len, ctx_len)


def diff_attention(geo, sc, lam_params, diff_g, lam_init, ctx_out):
    B, L, Lc, n_lat, n_all = geo["B"], geo["L"], geo["Lc"], geo["n_lat"], geo["n_all"]
    tq = _pick_tile((256, 128), L, Lc)
    nlq, ncq = L // tq, Lc // tq
    lat0, ctx0 = n_lat // tq, n_lat // Lc

    def orow(b, n):
        return jnp.where(n < nlq, b * nlq + n, lat0 + b * ncq + (n - nlq))

    return pl.pallas_call(
        functools.partial(_attn_c_kernel, L, Lc, tq, lam_init),
        grid=(B, nlq + (ncq if ctx_out else 0)),
        in_specs=[pl.BlockSpec((L, C_COLS), lambda b, n: (b, 0)),
                  pl.BlockSpec((Lc, C_COLS), lambda b, n: (ctx0 + b, 0)),
                  pl.BlockSpec((4, HEAD_DIM), lambda b, n: (0, 0)),
                  pl.BlockSpec((1, 2 * HEAD_DIM), lambda b, n: (0, 0))],
        out_specs=pl.BlockSpec((tq, BRANCH_W), lambda b, n: (orow(b, n), 0)),
        out_shape=jax.ShapeDtypeStruct((n_all, BRANCH_W), BF16),
        scratch_shapes=[pltpu.VMEM((L + Lc, BRANCH_W), BF16), pltpu.VMEM((L + Lc, BRANCH_W), BF16)],
        compiler_params=_cparams("parallel", "arbitrary"),
        name="diff_attention",
    )(sc, sc, lam_params, diff_g)


HALO = 16
CONV_ROWS = 64


def _local_kernel(tile, seq_len, ctx_len, n_lat_tiles, cur_ref, prev_ref, next_ref, cw_ref, cb_ref,
                  lg_ref, lb_ref, pw_ref, ps_ref, yb_ref, yd_ref, ypad, xpad):
    i = pl.program_id(0)
    tiles_per_seq = seq_len // tile
    is_lat = i < n_lat_tiles
    pos = jnp.where(is_lat, i % tiles_per_seq, 0)
    prev_ok = is_lat & (pos > 0)
    next_ok = is_lat & (pos < tiles_per_seq - 1)
    cc = BRANCH_W

    def glu(ref):
        return ref[:, 0:cc].astype(F32) * jax.nn.sigmoid(ref[:, cc:2 * cc].astype(F32))

    def pool_in(ref):
        return ref[:, 2 * cc:3 * cc].astype(F32)

    ypad[0:HALO, :] = jnp.where(prev_ok, glu(prev_ref), 0.0)
    ypad[HALO:HALO + tile, :] = glu(cur_ref)
    ypad[HALO + tile:2 * HALO + tile, :] = jnp.where(next_ok, glu(next_ref), 0.0)
    xpad[0:HALO, :] = jnp.where(prev_ok, pool_in(prev_ref), 0.0)
    xpad[HALO:HALO + tile, :] = pool_in(cur_ref)
    xpad[HALO + tile:2 * HALO + tile, :] = jnp.where(next_ok, pool_in(next_ref), 0.0)

    for r0 in range(0, tile, CONV_ROWS):
        acc = jnp.zeros((CONV_ROWS, cc), F32)
        for k in range(CONV_K):
            start = r0 + HALO - CONV_K // 2 + k
            acc = acc + cw_ref[k:k + 1, :] * ypad[start:start + CONV_ROWS, :]
        y = acc + cb_ref[...]
        mu = jnp.mean(y, axis=-1, keepdims=True)
        yc = y - mu
        var = jnp.mean(yc * yc, axis=-1, keepdims=True)
        z = yc * lax.rsqrt(var + EPS) * lg_ref[...] + lb_ref[...]
        yb_ref[r0:r0 + CONV_ROWS, :] = (z * jax.nn.sigmoid(z)).astype(yb_ref.dtype)

    gc = cc // len(POOL_WINDOWS)
    tpos = pos * tile + lax.broadcasted_iota(jnp.int32, (tile, 1), 0)
    last = jnp.where(is_lat, seq_len, ctx_len) - 1
    for g, win in enumerate(POOL_WINDOWS):
        lo, hi = win // 2, win - 1 - win // 2
        cols = slice(g * gc, (g + 1) * gc)
        wsum = jnp.zeros((tile, gc), F32)
        for j in range(-lo, hi + 1):
            wsum = wsum + xpad[HALO + j:HALO + j + tile, cols]
        cnt = (jnp.minimum(tpos + hi, last) - jnp.maximum(tpos - lo, 0) + 1).astype(F32)
        y = wsum / cnt - xpad[HALO:HALO + tile, cols]
        proj = jnp.dot(y.astype(BF16), pw_ref[g], preferred_element_type=F32)
        yd_ref[:, cols] = (proj * ps_ref[:, cols]).astype(yd_ref.dtype)


def local_mixers(geo, n_tokens, bd, conv_w, conv_b, ln_g, ln_b, pool_w, pool_scale):
    L, Lc, n_lat, n_all = geo["L"], geo["Lc"], geo["n_lat"], geo["n_all"]
    tile = Lc
    assert L % tile == 0 and tile % CONV_ROWS == 0 and tile % HALO == 0
    hb = tile // HALO
    n_halo_blocks = n_all // HALO
    cc = BRANCH_W
    vec = pl.BlockSpec((1, cc), lambda i: (0, 0))
    out = pl.BlockSpec((tile, cc), lambda i: (i, 0))
    return pl.pallas_call(
        functools.partial(_local_kernel, tile, L, Lc, n_lat // tile),
        grid=(n_tokens // tile,),
        in_specs=[pl.BlockSpec((tile, BD_COLS), lambda i: (i, 0)),
                  pl.BlockSpec((HALO, BD_COLS), lambda i: (jnp.maximum(i * hb - 1, 0), 0)),
                  pl.BlockSpec((HALO, BD_COLS), lambda i: (jnp.minimum((i + 1) * hb, n_halo_blocks - 1), 0)),
                  pl.BlockSpec((CONV_K, cc), lambda i: (0, 0)),
                  vec, vec, vec,
                  pl.BlockSpec((len(POOL_WINDOWS), cc // len(POOL_WINDOWS), cc // len(POOL_WINDOWS)),
                               lambda i: (0, 0, 0)),
                  vec],
        out_specs=[out, out],
        out_shape=[jax.ShapeDtypeStruct((n_all, cc), BF16)] * 2,
        scratch_shapes=[pltpu.VMEM((tile + 2 * HALO, cc), F32), pltpu.VMEM((tile + 2 * HALO, cc), F32)],
        compiler_params=_cparams("parallel"),
        name="local_mixers",
    )(bd, bd, bd, conv_w, conv_b, ln_g, ln_b, pool_w, pool_scale)


def _merge_kernel(ya_ref, yc_ref, yb_ref, yd_ref, gate_ref, x_ref, mod_ref, g_ref, wb_ref, wo_ref,
                  x1_ref, h2b_ref, h2f_ref):
    d = x_ref.shape[1]
    acc = None
    for n, y_ref in enumerate((ya_ref, yc_ref, yb_ref, yd_ref)):
        term = gate_ref[:, n * d:(n + 1) * d].astype(F32) * jnp.dot(y_ref[...], wb_ref[n],
                                                                   preferred_element_type=F32)
        acc = term if acc is None else acc + term
    out = jnp.dot(acc.astype(BF16), wo_ref[...], preferred_element_type=F32)
    x1 = x_ref[...] + mod_ref[0, 2:3, :] * out
    x1_ref[...] = x1
    h2 = _rms(x1) * g_ref[...] * (1.0 + mod_ref[0, 4:5, :]) + mod_ref[0, 3:4, :]
    h2f_ref[...] = h2
    h2b_ref[...] = h2.astype(BF16)


def merge_branches(geo, n_tokens, ya, yc, yb, yd, gates, x, mod, gain2, w_branch, w_out):
    n_all, D = x.shape
    tm = geo["tm_tok"]
    grp = geo["group_map"](tm)
    cc = BRANCH_W
    yspec = pl.BlockSpec((tm, cc), lambda i: (i, 0))
    row = pl.BlockSpec((tm, D), lambda i: (i, 0))
    return pl.pallas_call(
        _merge_kernel,
        grid=(n_tokens // tm,),
        in_specs=[yspec, yspec, yspec, yspec,
                  pl.BlockSpec((tm, N_BRANCH * D), lambda i: (i, 0)),
                  row,
                  pl.BlockSpec((1, 6, D), lambda i: (grp(i), 0, 0)),
                  pl.BlockSpec((1, D), lambda i: (0, 0)),
                  pl.BlockSpec((N_BRANCH, cc, D), lambda i: (0, 0, 0)),
                  pl.BlockSpec((D, D), lambda i: (0, 0))],
        out_specs=[row, row, row],
        out_shape=[jax.ShapeDtypeStruct((n_all, D), F32), jax.ShapeDtypeStruct((n_all, D), BF16),
                   jax.ShapeDtypeStruct((n_all, D), F32)],
        compiler_params=_cparams("parallel"),
        name="merge_branches",
    )(ya, yc, yb, yd, gates, x, mod, gain2, w_branch, w_out)


_CAND_PAIRS = [(a, b) for a in range(PEER_TOPK) for b in range(PEER_TOPK) if (a + 1) * (b + 1) <= PEER_TOPK]
_CAND_ROWS = -(-len(_CAND_PAIRS) // 8) * 8


def _extract_top(s, row, n_rows):
    m = jnp.max(s, axis=0, keepdims=True)
    am = jnp.min(jnp.where(s == m, row, n_rows), axis=0, keepdims=True)
    return m, am


def _route_kernel(h_ref, wq_ref, k1_ref, k2_ref, idx_ref, gate_ref,
                  v1_s, i1_s, v2_s, i2_s, cand_s, cidx_s, top_s, idxt_s, gatet_s):
    tm = h_ref.shape[0]
    q = jnp.dot(h_ref[...], wq_ref[...], preferred_element_type=F32).astype(BF16)
    nt = (((1,), (1,)), ((), ()))
    key_row = lax.broadcasted_iota(jnp.int32, (PEER_NKEYS, tm), 0)
    cand_row = lax.broadcasted_iota(jnp.int32, (_CAND_ROWS, tm), 0)
    if _CAND_ROWS > len(_CAND_PAIRS):
        cand_s[len(_CAND_PAIRS):_CAND_ROWS, :] = jnp.full((_CAND_ROWS - len(_CAND_PAIRS), tm), NEG_BIG, F32)
        cidx_s[len(_CAND_PAIRS):_CAND_ROWS, :] = jnp.zeros((_CAND_ROWS - len(_CAND_PAIRS), tm), jnp.int32)
    for h in range(PEER_HEADS):
        for c, (k_ref, v_s, i_s) in enumerate(((k1_ref, v1_s, i1_s), (k2_ref, v2_s, i2_s))):
            col = (h * 2 + c) * PEER_NKEYS
            s = lax.dot_general(k_ref[h], q[:, col:col + PEER_NKEYS], nt, preferred_element_type=F32)
            for r in range(PEER_TOPK):
                m, am = _extract_top(s, key_row, PEER_NKEYS)
                v_s[r:r + 1, :] = m
                i_s[r:r + 1, :] = am
                s = jnp.where(key_row == am, NEG_BIG, s)
        a_prev, off = -1, 0
        for a in range(PEER_TOPK):
            nb_a = PEER_TOPK // (a + 1)
            cand_s[off:off + nb_a, :] = v1_s[a:a + 1, :] + v2_s[0:nb_a, :]
            cidx_s[off:off + nb_a, :] = i1_s[a:a + 1, :] * PEER_NKEYS + i2_s[0:nb_a, :]
            off += nb_a
        cand = cand_s[...]
        cidx = cidx_s[...]
        for r in range(PEER_TOPK):
            m, am = _extract_top(cand, cand_row, _CAND_ROWS)
            hit = cand_row == am
            top_s[r:r + 1, :] = m
            idxt_s[h * PEER_TOPK + r:h * PEER_TOPK + r + 1, :] = jnp.sum(jnp.where(hit, cidx, 0), axis=0,
                                                                          keepdims=True)
            cand = jnp.where(hit, NEG_BIG, cand)
        top = top_s[...]
        e = jnp.exp(top - jnp.max(top, axis=0, keepdims=True))
        gatet_s[h * PEER_TOPK:(h + 1) * PEER_TOPK, :] = e / jnp.sum(e, axis=0, keepdims=True)
    idx_ref[...] = idxt_s[...].T
    gate_ref[...] = gatet_s[...].T


def peer_route(geo, n_tokens, h2, wq, k1, k2):
    n_all, D = h2.shape
    tm = _pick_tile((256, 128), geo["n_lat"], geo["n_ctx"])
    qw = wq.shape[1]
    kspec = pl.BlockSpec((PEER_HEADS, PEER_NKEYS, qw // (2 * PEER_HEADS)), lambda i: (0, 0, 0))
    out = pl.BlockSpec((tm, PEER_ROWS), lambda i: (i, 0))
    return pl.pallas_call(
        _route_kernel,
        grid=(n_tokens // tm,),
        in_specs=[pl.BlockSpec((tm, D), lambda i: (i, 0)),
                  pl.BlockSpec((D, qw), lambda i: (0, 0)),
                  kspec, kspec],
        out_specs=[out, out],
        out_shape=[jax.ShapeDtypeStruct((n_all, PEER_ROWS), jnp.int32),
                   jax.ShapeDtypeStruct((n_all, PEER_ROWS), F32)],
        scratch_shapes=[pltpu.VMEM((PEER_TOPK, tm), F32), pltpu.VMEM((PEER_TOPK, tm), jnp.int32),
                        pltpu.VMEM((PEER_TOPK, tm), F32), pltpu.VMEM((PEER_TOPK, tm), jnp.int32),
                        pltpu.VMEM((_CAND_ROWS, tm), F32), pltpu.VMEM((_CAND_ROWS, tm), jnp.int32),
                        pltpu.VMEM((PEER_TOPK, tm), F32),
                        pltpu.VMEM((PEER_ROWS, tm), jnp.int32), pltpu.VMEM((PEER_ROWS, tm), F32)],
        compiler_params=_cparams("parallel"),
        name="peer_route",
    )(h2, wq, k1, k2)


def _act_kernel(gate_ref, dots_ref, o_ref):
    d = dots_ref[...]
    o_ref[...] = gate_ref[...] * (0.5 * d * (1.0 + lax.erf(d * (2.0 ** -0.5))))


def peer_activation(geo, n_tokens, gate, dots):
    tm = geo["tm_mm"]
    spec = pl.BlockSpec((tm, PEER_ROWS), lambda i: (i, 0))
    return pl.pallas_call(
        _act_kernel,
        grid=(n_tokens // tm,),
        in_specs=[spec, spec],
        out_specs=spec,
        out_shape=jax.ShapeDtypeStruct((n_tokens, PEER_ROWS), F32),
        compiler_params=_cparams("parallel"),
        name="peer_activation",
    )(gate, dots)


def _rope_tables(seq_len, pad_rows, rope_cols):
    rows = seq_len // GRID_W
    row = jnp.repeat(jnp.arange(rows, dtype=F32), GRID_W)
    col = jnp.tile(jnp.arange(GRID_W, dtype=F32), rows)
    n_freq = HEAD_DIM // 4
    inv_freq = ROPE_THETA ** (-jnp.arange(n_freq, dtype=F32) / n_freq)
    ang = jnp.concatenate([row[:, None] * inv_freq, col[:, None] * inv_freq], axis=-1)
    cos_h = jnp.concatenate([jnp.cos(ang), jnp.cos(ang)], axis=-1)
    sin_h = jnp.concatenate([-jnp.sin(ang), jnp.sin(ang)], axis=-1)
    one, zero = jnp.ones_like(cos_h), jnp.zeros_like(sin_h)
    cos = jnp.concatenate([cos_h if r else one for r in rope_cols], axis=-1)
    sin = jnp.concatenate([sin_h if r else zero for r in rope_cols], axis=-1)
    w = cos.shape[1]
    cos = jnp.concatenate([cos, jnp.ones((pad_rows, w), F32)], axis=0)
    sin = jnp.concatenate([sin, jnp.zeros((pad_rows, w), F32)], axis=0)
    return cos, sin


def kernel(x, c, ctx, c_ctx, w_mod, b_mod, norm1_g, norm2_g, w_in, w_gate, b_gate, attn_sink, lam_q1, lam_k1, lam_q2, lam_k2, diff_norm_g, conv_w, conv_b, conv_ln_g, conv_ln_b, pool_w, pool_scale, w_branch, w_out, peer_wq, peer_k1, peer_k2, peer_u, peer_v, final_g):
    B, L, D = x.shape
    Lc = ctx.shape[1]
    depth = w_in.shape[0]
    n_lat, n_ctx = B * L, B * Lc
    n_all = n_lat + n_ctx
    assert L % GRID_W == 0 and L % BLOCK == 0 and Lc % BLOCK == 0 and n_lat % Lc == 0

    tm_mm = _pick_tile((1024, 512, 256, 128), L, n_ctx)
    tm_tok = _pick_tile((512, 256, 128), L, n_ctx)
    geo = dict(
        B=B, L=L, Lc=Lc, n_lat=n_lat, n_ctx=n_ctx, n_all=n_all, tm_mm=tm_mm, tm_tok=tm_tok,
        group_map=lambda tm: (lambda i: jnp.where(i < n_lat // tm, (i * tm) // L, B)),
        pos_map=lambda tm: (lambda i: jnp.where(i < n_lat // tm, i % (L // tm), L // tm)),
    )

    xa = jnp.concatenate([x.reshape(n_lat, D), ctx.reshape(n_ctx, D)], axis=0)
    n_mod_rows = -(-(B + 1) // 8) * 8
    cvec = jnp.concatenate([c, c_ctx[None, :], jnp.zeros((n_mod_rows - B - 1, D), F32)], axis=0)

    head_a = [True] * WA_HEADS + [True] * WA_KV_HEADS + [False] * WA_KV_HEADS
    head_c = [True] * (2 * BRANCH_W // HEAD_DIM) + [False] * (BRANCH_W // HEAD_DIM)
    cos_a, sin_a = _rope_tables(L, tm_mm, head_a)
    cos_c, sin_c = _rope_tables(L, tm_mm, head_c)

    ffn = None
    mod_prev = None
    for i in range(depth):
        ctx_out = i < depth - 1
        n_tok = n_all if ctx_out else n_lat
        lam_init = 0.8 - 0.6 * math.exp(-0.3 * i)
        mod = mod_vectors(cvec, w_mod[i], b_mod[i]).reshape(n_mod_rows, 6, D)

        if ffn is None:
            (h,) = pre_norm(geo, n_all, xa, None, None, mod, norm1_g[i][None, :])
        else:
            xa, h = pre_norm(geo, n_all, xa, ffn, mod_prev, mod, norm1_g[i][None, :])

        w_in_b = w_in[i].astype(BF16)
        sa = token_matmul(geo, n_all, h, w_in_b[:, :A_COLS], "rope", (cos_a, sin_a))
        sc = token_matmul(geo, n_all, h, w_in_b[:, A_COLS:A_COLS + C_COLS], "rope", (cos_c, sin_c))
        ya = window_attention(geo, sa, attn_sink[i], ctx_out)
        lam_params = jnp.stack([lam_q1[i], lam_k1[i], lam_q2[i], lam_k2[i]], axis=0)
        yc = diff_attention(geo, sc, lam_params, diff_norm_g[i][None, :], lam_init, ctx_out)
        bd = token_matmul(geo, n_tok, h, w_in_b[:, A_COLS + C_COLS:])
        yb, yd = local_mixers(geo, n_tok, bd, conv_w[i], conv_b[i][None, :], conv_ln_g[i][None, :],
                              conv_ln_b[i][None, :], pool_w[i].astype(BF16), pool_scale[i][None, :])
        wg = jnp.transpose(w_gate[i], (1, 0, 2)).reshape(D, N_BRANCH * D).astype(BF16)
        gates = token_matmul(geo, n_tok, h, wg, "sigmoid", (b_gate[i].reshape(1, N_BRANCH * D),))
        xa, h2b, h2f = merge_branches(geo, n_tok, ya, yc, yb, yd, gates, xa, mod, norm2_g[i][None, :],
                                      w_branch[i].astype(BF16), w_out[i].astype(BF16))
        idx, gate = peer_route(geo, n_tok, h2b, peer_wq[i].astype(BF16), peer_k1[i].astype(BF16),
                               peer_k2[i].astype(BF16))
        dots = sc_peer_dots(h2f, idx, peer_u[i], n_tok)
        wgt = peer_activation(geo, n_tok, gate, dots)
        ffn = sc_peer_wsum(wgt, idx, peer_v[i], n_tok)
        mod_prev = mod

    (out,) = pre_norm(geo, n_lat, xa, ffn, mod_prev, mod_prev, final_g[None, :], final=True)
    return out.reshape(B, L, D)
```

```python
import functools
import math

import jax
import jax.numpy as jnp
from jax import lax
from jax.experimental import pallas as pl
from jax.experimental.pallas import tpu as pltpu
from jax.experimental.pallas import tpu_sc as plsc

GRID_W = 64
HEAD_DIM = 64
ROPE_THETA = 10000.0
BLOCK = 128
WINDOW = 128
N_BRANCH = 4
WA_HEADS = 8
WA_KV_HEADS = 2
WA_GROUP = WA_HEADS // WA_KV_HEADS
DIFF_HEADS = 4
CONV_K = 31
POOL_WINDOWS = (2, 4, 8, 16)
PEER_HEADS = 8
PEER_NKEYS = 128
PEER_TOPK = 16
EPS = 1e-6
NEG_INF = -1e30
NEG_BIG = -3.0e38
BRANCH_W = WA_HEADS * HEAD_DIM
A_COLS = BRANCH_W + 2 * WA_KV_HEADS * HEAD_DIM
C_COLS = 3 * BRANCH_W
BD_COLS = 3 * BRANCH_W

LANES = 128
VMEM_LIMIT = 48 * 1024 * 1024
SC_LANES = 16
SC_WORKERS = 32
PEER_ROWS = PEER_HEADS * PEER_TOPK
GATHER_ROWS = 32
GATHER_BUFFERS = 3
SC_TOKEN_BLOCK = 16
N_STREAMS = 2

F32 = jnp.float32
BF16 = jnp.bfloat16


def _cparams(*sem):
    return pltpu.CompilerParams(dimension_semantics=sem, vmem_limit_bytes=VMEM_LIMIT)


def _pick_tile(candidates, *sizes):
    for t in candidates:
        if all(s % t == 0 for s in sizes):
            return t
    raise ValueError(f"no tile in {candidates} divides {sizes}")


def _rms(x):
    return x * lax.rsqrt(jnp.mean(x * x, axis=-1, keepdims=True) + EPS)


def _sc_mesh():
    return plsc.VectorSubcoreMesh(core_axis_name="c", subcore_axis_name="s")


def _sc_worker_id():
    return lax.axis_index("s") * 2 + lax.axis_index("c")


def _lane_broadcast(vec, lane, r):
    return jnp.full((SC_LANES,), jnp.sum(jnp.where(lane == r, vec, 0.0)), F32)


def _tree_sum(parts):
    while len(parts) > 1:
        parts = [parts[i] + parts[i + 1] for i in range(0, len(parts), 2)]
    return parts[0]


def _sc_peer_call(kind, a, idx, table, n_tokens):
    T = n_tokens
    D = table.shape[1]
    TB, GR, NBUF = SC_TOKEN_BLOCK, GATHER_ROWS, GATHER_BUFFERS
    assert T % (SC_WORKERS * TB) == 0 and D % SC_LANES == 0 and idx.shape[1] == PEER_ROWS
    tpw = T // SC_WORKERS
    n_chunks = D // SC_LANES
    n_groups = PEER_ROWS // GR
    n_steps = TB * n_groups
    a_width = a.shape[1]
    out_width = PEER_ROWS if kind == "dots" else D

    def body(a_hbm, idx_hbm, tab_hbm, out_hbm, idx_v, a_v, rows_v, out_v, sems):
        wid = _sc_worker_id()
        lane = lax.iota(jnp.int32, SC_LANES)

        def gather(step):
            t, g, slot = step // n_groups, step % n_groups, step % NBUF
            return pltpu.make_async_copy(tab_hbm.at[idx_v.at[t, pl.ds(g * GR, GR)]],
                                         rows_v.at[pl.ds(slot * GR, GR)], sems.at[slot])

        def dots_group(t, g, base):
            for half in range(GR // SC_LANES):
                r0 = base + half * SC_LANES

                def chunk(c, accs):
                    sl = pl.ds(c * SC_LANES, SC_LANES)
                    xc = a_v[t, sl]
                    return tuple(acc + xc * rows_v[r0 + r, sl] for r, acc in enumerate(accs))

                accs = plsc.parallel_loop(0, n_chunks, 1, unroll=2,
                                          carry=tuple(jnp.zeros((SC_LANES,), F32) for _ in range(SC_LANES)))(chunk)
                tot = jnp.zeros((SC_LANES,), F32)
                for r in range(SC_LANES):
                    tot = jnp.where(lane == r, jnp.sum(accs[r]), tot)
                out_v[t, pl.ds(g * GR + half * SC_LANES, SC_LANES)] = tot

        def wsum_group(t, g, base):
            for half in range(GR // SC_LANES):
                r0 = base + half * SC_LANES
                wvec = a_v[t, pl.ds(g * GR + half * SC_LANES, SC_LANES)]
                wb = [_lane_broadcast(wvec, lane, r) for r in range(SC_LANES)]
                first = (g == 0 and half == 0)

                @plsc.parallel_loop(0, n_chunks, 1, unroll=4)
                def _(c):
                    sl = pl.ds(c * SC_LANES, SC_LANES)
                    parts = [wb[r] * rows_v[r0 + r, sl] + wb[r + 1] * rows_v[r0 + r + 1, sl]
                             for r in range(0, SC_LANES, 2)]
                    s = _tree_sum(parts)
                    out_v[t, sl] = s if first else out_v[t, sl] + s

        group = dots_group if kind == "dots" else wsum_group

        @pl.loop(0, tpw // TB)
        def _(b):
            tok0 = wid * tpw + b * TB
            pltpu.sync_copy(idx_hbm.at[pl.ds(tok0, TB)], idx_v)
            pltpu.sync_copy(a_hbm.at[pl.ds(tok0, TB)], a_v)
            for p in range(NBUF - 1):
                gather(p).start()

            @pl.loop(0, TB)
            def _(t):
                for g in range(n_groups):
                    step = t * n_groups + g

                    @pl.when(step + NBUF - 1 < n_steps)
                    def _():
                        gather(step + NBUF - 1).start()

                    gather(step).wait()
                    group(t, g, (step % NBUF) * GR)

            pltpu.sync_copy(out_v, out_hbm.at[pl.ds(tok0, TB)])

    f = pl.kernel(
        body,
        out_type=jax.ShapeDtypeStruct((T, out_width), F32),
        mesh=_sc_mesh(),
        compiler_params=pltpu.CompilerParams(needs_layout_passes=False),
        scratch_types=[
            pltpu.VMEM((TB, PEER_ROWS), jnp.int32),
            pltpu.VMEM((TB, a_width), F32),
            pltpu.VMEM((NBUF * GR, D), F32),
            pltpu.VMEM((TB, out_width), F32),
            pltpu.SemaphoreType.DMA((NBUF,)),
        ],
    )
    return f(a, idx, table)


def sc_peer_dots(x, idx, table, n_tokens):
    return _sc_peer_call("dots", x, idx, table, n_tokens)


def sc_peer_wsum(w, idx, table, n_tokens):
    return _sc_peer_call("wsum", w, idx, table, n_tokens)


def _mod_kernel(c_ref, w_ref, b_ref, o_ref):
    cv = c_ref[...]
    s = cv * jax.nn.sigmoid(cv)
    o_ref[...] = jnp.dot(s, w_ref[...], precision=lax.Precision.HIGHEST,
                         preferred_element_type=F32) + b_ref[...]


def mod_vectors(cvec, w_mod, b_mod):
    R, D = cvec.shape
    W = w_mod.shape[1]
    tn = _pick_tile((512, 256, 128), W)
    return pl.pallas_call(
        _mod_kernel,
        grid=(W // tn,),
        in_specs=[pl.BlockSpec((R, D), lambda j: (0, 0)),
                  pl.BlockSpec((D, tn), lambda j: (0, j)),
                  pl.BlockSpec((1, tn), lambda j: (0, j))],
        out_specs=pl.BlockSpec((R, tn), lambda j: (0, j)),
        out_shape=jax.ShapeDtypeStruct((R, W), F32),
        compiler_params=_cparams("arbitrary"),
        name="mod_vectors",
    )(cvec, w_mod, b_mod.reshape(1, W))


def _pre_kernel(has_ffn, final, *refs):
    if has_ffn:
        x_ref, ffn_ref, modp_ref, mod_ref, g_ref = refs[:5]
        outs = refs[5:]
        x = x_ref[...] + modp_ref[0, 5:6, :] * ffn_ref[...]
    else:
        x_ref, mod_ref, g_ref = refs[:3]
        outs = refs[3:]
        x = x_ref[...]
    y = _rms(x) * g_ref[...]
    if final:
        outs[0][...] = y
        return
    h = y * (1.0 + mod_ref[0, 1:2, :]) + mod_ref[0, 0:1, :]
    if has_ffn:
        outs[0][...] = x
        outs[1][...] = h.astype(BF16)
    else:
        outs[0][...] = h.astype(BF16)


def pre_norm(geo, n_tokens, x, ffn, mod_prev, mod, gain, final=False):
    D = x.shape[1]
    tm = geo["tm_tok"]
    grp = geo["group_map"](tm)
    has_ffn = ffn is not None
    row = pl.BlockSpec((tm, D), lambda i: (i, 0))
    modspec = pl.BlockSpec((1, 6, D), lambda i: (grp(i), 0, 0))
    gspec = pl.BlockSpec((1, D), lambda i: (0, 0))
    if has_ffn:
        args = (x, ffn, mod_prev, mod, gain)
        in_specs = [row, row, modspec, modspec, gspec]
    else:
        args = (x, mod, gain)
        in_specs = [row, modspec, gspec]
    n_all = x.shape[0]
    if final:
        out_shape = [jax.ShapeDtypeStruct((n_tokens, D), F32)]
    elif has_ffn:
        out_shape = [jax.ShapeDtypeStruct((n_all, D), F32), jax.ShapeDtypeStruct((n_all, D), BF16)]
    else:
        out_shape = [jax.ShapeDtypeStruct((n_all, D), BF16)]
    outs = pl.pallas_call(
        functools.partial(_pre_kernel, has_ffn, final),
        grid=(n_tokens // tm,),
        in_specs=in_specs,
        out_specs=[row] * len(out_shape),
        out_shape=out_shape,
        compiler_params=_cparams("parallel"),
        name="pre_norm",
    )(*args)
    return outs


def _mm_kernel(epi, tn, a_ref, w_ref, *rest):
    acc = jnp.dot(a_ref[...], w_ref[...], preferred_element_type=F32)
    if epi == "rope":
        cos_ref, sin_ref, o_ref = rest
        for ch in range(tn // LANES):
            sl = slice(ch * LANES, (ch + 1) * LANES)
            a = acc[:, sl]
            lane = lax.broadcasted_iota(jnp.int32, a.shape, 1)
            partner = jnp.where((lane % HEAD_DIM) < HEAD_DIM // 2,
                                pltpu.roll(a, LANES - HEAD_DIM // 2, 1), pltpu.roll(a, HEAD_DIM // 2, 1))
            o_ref[:, sl] = (a * cos_ref[:, sl] + partner * sin_ref[:, sl]).astype(o_ref.dtype)
    elif epi == "sigmoid":
        b_ref, o_ref = rest
        o_ref[...] = jax.nn.sigmoid(acc + b_ref[...]).astype(o_ref.dtype)
    else:
        (o_ref,) = rest
        o_ref[...] = acc.astype(o_ref.dtype)


def token_matmul(geo, n_tokens, a, w, epi="none", extras=()):
    n_all, K = a.shape
    W = w.shape[1]
    tm = geo["tm_mm"]
    tn = _pick_tile((768, 512, 256, 128), W)
    in_specs = [pl.BlockSpec((tm, K), lambda i, j: (i, 0)), pl.BlockSpec((K, tn), lambda i, j: (0, j))]
    if epi == "rope":
        pos = geo["pos_map"](tm)
        in_specs += [pl.BlockSpec((tm, tn), lambda i, j: (pos(i), j))] * 2
    elif epi == "sigmoid":
        in_specs += [pl.BlockSpec((1, tn), lambda i, j: (0, j))]
    return pl.pallas_call(
        functools.partial(_mm_kernel, epi, tn),
        grid=(n_tokens // tm, W // tn),
        in_specs=in_specs,
        out_specs=pl.BlockSpec((tm, tn), lambda i, j: (i, j)),
        out_shape=jax.ShapeDtypeStruct((n_all, W), BF16),
        compiler_params=_cparams("parallel", "arbitrary"),
        name="token_matmul_" + epi,
    )(a, w, *extras)


def _attn_a_kernel(nb, seq_len, sink_ref, cur_ref, prev_ref, next_ref, ctx_ref, o_ref):
    n = pl.program_id(1)
    is_lat = n < nb
    qi = lax.broadcasted_iota(jnp.int32, (BLOCK, 3 * BLOCK), 0)
    kj = lax.broadcasted_iota(jnp.int32, (BLOCK, 3 * BLOCK), 1)
    kabs = (n - 1) * BLOCK + kj
    band = (jnp.abs(kj - BLOCK - qi) <= WINDOW) & (kabs >= 0) & (kabs < seq_len) & is_lat
    nt = (((1,), (1,)), ((), ()))
    kv_off = WA_HEADS * HEAD_DIM
    for kvh in range(WA_KV_HEADS):
        kc = slice(kv_off + kvh * HEAD_DIM, kv_off + (kvh + 1) * HEAD_DIM)
        vc = slice(kv_off + (WA_KV_HEADS + kvh) * HEAD_DIM, kv_off + (WA_KV_HEADS + kvh + 1) * HEAD_DIM)
        k_lat = jnp.concatenate([prev_ref[:, kc], cur_ref[:, kc], next_ref[:, kc]], axis=0)
        v_lat = jnp.concatenate([prev_ref[:, vc], cur_ref[:, vc], next_ref[:, vc]], axis=0)
        k_ctx = ctx_ref[:, kc]
        v_ctx = ctx_ref[:, vc]
        for g in range(WA_GROUP):
            h = kvh * WA_GROUP + g
            qh = cur_ref[:, h * HEAD_DIM:(h + 1) * HEAD_DIM]
            s1 = lax.dot_general(qh, k_lat, nt, preferred_element_type=F32) * (HEAD_DIM ** -0.5)
            s1 = jnp.where(band, s1, NEG_INF)
            s2 = lax.dot_general(qh, k_ctx, nt, preferred_element_type=F32) * (HEAD_DIM ** -0.5)
            sink = sink_ref[h]
            m = jnp.maximum(jnp.maximum(jnp.max(s1, axis=1, keepdims=True),
                                        jnp.max(s2, axis=1, keepdims=True)), sink)
            e1 = jnp.exp(s1 - m)
            e2 = jnp.exp(s2 - m)
            den = jnp.sum(e1, axis=1, keepdims=True) + jnp.sum(e2, axis=1, keepdims=True) + jnp.exp(sink - m)
            pv = (jnp.dot(e1.astype(BF16), v_lat, preferred_element_type=F32)
                  + jnp.dot(e2.astype(BF16), v_ctx, preferred_element_type=F32))
            o_ref[:, h * HEAD_DIM:(h + 1) * HEAD_DIM] = (pv / den).astype(o_ref.dtype)


def window_attention(geo, sa, sink, ctx_out):
    B, L, Lc, n_lat, n_all = geo["B"], geo["L"], geo["Lc"], geo["n_lat"], geo["n_all"]
    nb, ncb = L // BLOCK, Lc // BLOCK
    lat0, ctx0 = n_lat // BLOCK, n_lat // Lc

    def cur(b, n):
        return jnp.where(n < nb, b * nb + n, lat0 + b * ncb + (n - nb))

    def prev(b, n):
        return jnp.where(n < nb, b * nb + jnp.maximum(n - 1, 0), lat0 + b * ncb + (n - nb))

    def nxt(b, n):
        return jnp.where(n < nb, b * nb + jnp.minimum(n + 1, nb - 1), lat0 + b * ncb + (n - nb))

    blk = (BLOCK, A_COLS)
    return pl.pallas_call(
        functools.partial(_attn_a_kernel, nb, L),
        grid=(B, nb + (ncb if ctx_out else 0)),
        in_specs=[pl.BlockSpec(memory_space=pltpu.SMEM),
                  pl.BlockSpec(blk, lambda b, n: (cur(b, n), 0)),
                  pl.BlockSpec(blk, lambda b, n: (prev(b, n), 0)),
                  pl.BlockSpec(blk, lambda b, n: (nxt(b, n), 0)),
                  pl.BlockSpec((Lc, A_COLS), lambda b, n: (ctx0 + b, 0))],
        out_specs=pl.BlockSpec((BLOCK, BRANCH_W), lambda b, n: (cur(b, n), 0)),
        out_shape=jax.ShapeDtypeStruct((n_all, BRANCH_W), BF16),
        compiler_params=_cparams("parallel", "arbitrary"),
        name="window_attention",
    )(sink, sa, sa, sa, sa)


def _attn_c_kernel(seq_len, ctx_len, tq, lam_init, lat_ref, ctx_ref, lam_ref, g_ref, o_ref, kall, vall):
    n = pl.program_id(1)
    nlq = seq_len // tq
    kcol = slice(BRANCH_W, 2 * BRANCH_W)
    vcol = slice(2 * BRANCH_W, 3 * BRANCH_W)

    @pl.when(n == 0)
    def _():
        kall[0:seq_len, :] = lat_ref[:, kcol]
        kall[seq_len:seq_len + ctx_len, :] = ctx_ref[:, kcol]
        vall[0:seq_len, :] = lat_ref[:, vcol]
        vall[seq_len:seq_len + ctx_len, :] = ctx_ref[:, vcol]

    lam = (jnp.exp(jnp.sum(lam_ref[0:1, :] * lam_ref[1:2, :], axis=1, keepdims=True))
           - jnp.exp(jnp.sum(lam_ref[2:3, :] * lam_ref[3:4, :], axis=1, keepdims=True)) + lam_init)
    nt = (((1,), (1,)), ((), ()))

    def heads(q, k0, klen):
        for h in range(DIFF_HEADS):
            probs = []
            for c in range(2):
                col = slice(h * 2 * HEAD_DIM + c * HEAD_DIM, h * 2 * HEAD_DIM + (c + 1) * HEAD_DIM)
                s = lax.dot_general(q[:, col], kall[k0:k0 + klen, col], nt,
                                    preferred_element_type=F32) * (HEAD_DIM ** -0.5)
                e = jnp.exp(s - jnp.max(s, axis=1, keepdims=True))
                probs.append(e * (1.0 / jnp.sum(e, axis=1, keepdims=True)))
            a = probs[0] - lam * probs[1]
            hv = slice(h * 2 * HEAD_DIM, (h + 1) * 2 * HEAD_DIM)
            o = jnp.dot(a.astype(BF16), vall[k0:k0 + klen, hv], preferred_element_type=F32)
            o_ref[:, hv] = (_rms(o) * g_ref[...] * (1.0 - lam_init)).astype(o_ref.dtype)

    @pl.when(n < nlq)
    def _():
        heads(lat_ref[pl.ds(pl.multiple_of(n * tq, tq), tq), 0:BRANCH_W], 0, seq_len + ctx_len)

    @pl.when(n >= nlq)
    def _():
        heads(ctx_ref[pl.ds(pl.multiple_of((n - nlq) * tq, tq), tq), 0:BRANCH_W], seq_len, ctx_len)


def diff_attention(geo, sc, lam_params, diff_g, lam_init, ctx_out):
    B, L, Lc, n_lat, n_all = geo["B"], geo["L"], geo["Lc"], geo["n_lat"], geo["n_all"]
    tq = _pick_tile((256, 128), L, Lc)
    nlq, ncq = L // tq, Lc // tq
    lat0, ctx0 = n_lat // tq, n_lat // Lc

    def orow(b, n):
        return jnp.where(n < nlq, b * nlq + n, lat0 + b * ncq + (n - nlq))

    return pl.pallas_call(
        functools.partial(_attn_c_kernel, L, Lc, tq, lam_init),
        grid=(B, nlq + (ncq if ctx_out else 0)),
        in_specs=[pl.BlockSpec((L, C_COLS), lambda b, n: (b, 0)),
                  pl.BlockSpec((Lc, C_COLS), lambda b, n: (ctx0 + b, 0)),
                  pl.BlockSpec((4, HEAD_DIM), lambda b, n: (0, 0)),
                  pl.BlockSpec((1, 2 * HEAD_DIM), lambda b, n: (0, 0))],
        out_specs=pl.BlockSpec((tq, BRANCH_W), lambda b, n: (orow(b, n), 0)),
        out_shape=jax.ShapeDtypeStruct((n_all, BRANCH_W), BF16),
        scratch_shapes=[pltpu.VMEM((L + Lc, BRANCH_W), BF16), pltpu.VMEM((L + Lc, BRANCH_W), BF16)],
        compiler_params=_cparams("parallel", "arbitrary"),
        name="diff_attention",
    )(sc, sc, lam_params, diff_g)


HALO = 16
CONV_ROWS = 64


def _local_kernel(tile, seq_len, ctx_len, n_lat_tiles, cur_ref, prev_ref, next_ref, cw_ref, cb_ref,
                  lg_ref, lb_ref, pw_ref, ps_ref, yb_ref, yd_ref, ypad, xpad):
    i = pl.program_id(0)
    tiles_per_seq = seq_len // tile
    is_lat = i < n_lat_tiles
    pos = jnp.where(is_lat, i % tiles_per_seq, 0)
    prev_ok = is_lat & (pos > 0)
    next_ok = is_lat & (pos < tiles_per_seq - 1)
    cc = BRANCH_W

    def glu(ref):
        return ref[:, 0:cc].astype(F32) * jax.nn.sigmoid(ref[:, cc:2 * cc].astype(F32))

    def pool_in(ref):
        return ref[:, 2 * cc:3 * cc].astype(F32)

    ypad[0:HALO, :] = jnp.where(prev_ok, glu(prev_ref), 0.0)
    ypad[HALO:HALO + tile, :] = glu(cur_ref)
    ypad[HALO + tile:2 * HALO + tile, :] = jnp.where(next_ok, glu(next_ref), 0.0)
    xpad[0:HALO, :] = jnp.where(prev_ok, pool_in(prev_ref), 0.0)
    xpad[HALO:HALO + tile, :] = pool_in(cur_ref)
    xpad[HALO + tile:2 * HALO + tile, :] = jnp.where(next_ok, pool_in(next_ref), 0.0)

    for r0 in range(0, tile, CONV_ROWS):
        acc = jnp.zeros((CONV_ROWS, cc), F32)
        for k in range(CONV_K):
            start = r0 + HALO - CONV_K // 2 + k
            acc = acc + cw_ref[k:k + 1, :] * ypad[start:start + CONV_ROWS, :]
        y = acc + cb_ref[...]
        mu = jnp.mean(y, axis=-1, keepdims=True)
        yc = y - mu
        var = jnp.mean(yc * yc, axis=-1, keepdims=True)
        z = yc * lax.rsqrt(var + EPS) * lg_ref[...] + lb_ref[...]
        yb_ref[r0:r0 + CONV_ROWS, :] = (z * jax.nn.sigmoid(z)).astype(yb_ref.dtype)

    gc = cc // len(POOL_WINDOWS)
    tpos = pos * tile + lax.broadcasted_iota(jnp.int32, (tile, 1), 0)
    last = jnp.where(is_lat, seq_len, ctx_len) - 1
    for g, win in enumerate(POOL_WINDOWS):
        lo, hi = win // 2, win - 1 - win // 2
        cols = slice(g * gc, (g + 1) * gc)
        wsum = jnp.zeros((tile, gc), F32)
        for j in range(-lo, hi + 1):
            wsum = wsum + xpad[HALO + j:HALO + j + tile, cols]
        cnt = (jnp.minimum(tpos + hi, last) - jnp.maximum(tpos - lo, 0) + 1).astype(F32)
        y = wsum / cnt - xpad[HALO:HALO + tile, cols]
        proj = jnp.dot(y.astype(BF16), pw_ref[g], preferred_element_type=F32)
        yd_ref[:, cols] = (proj * ps_ref[:, cols]).astype(yd_ref.dtype)


def local_mixers(geo, n_tokens, bd, conv_w, conv_b, ln_g, ln_b, pool_w, pool_scale):
    L, Lc, n_lat, n_all = geo["L"], geo["Lc"], geo["n_lat"], geo["n_all"]
    tile = Lc
    assert L % tile == 0 and tile % CONV_ROWS == 0 and tile % HALO == 0
    hb = tile // HALO
    n_halo_blocks = n_all // HALO
    cc = BRANCH_W
    vec = pl.BlockSpec((1, cc), lambda i: (0, 0))
    out = pl.BlockSpec((tile, cc), lambda i: (i, 0))
    return pl.pallas_call(
        functools.partial(_local_kernel, tile, L, Lc, n_lat // tile),
        grid=(n_tokens // tile,),
        in_specs=[pl.BlockSpec((tile, BD_COLS), lambda i: (i, 0)),
                  pl.BlockSpec((HALO, BD_COLS), lambda i: (jnp.maximum(i * hb - 1, 0), 0)),
                  pl.BlockSpec((HALO, BD_COLS), lambda i: (jnp.minimum((i + 1) * hb, n_halo_blocks - 1), 0)),
                  pl.BlockSpec((CONV_K, cc), lambda i: (0, 0)),
                  vec, vec, vec,
                  pl.BlockSpec((len(POOL_WINDOWS), cc // len(POOL_WINDOWS), cc // len(POOL_WINDOWS)),
                               lambda i: (0, 0, 0)),
                  vec],
        out_specs=[out, out],
        out_shape=[jax.ShapeDtypeStruct((n_all, cc), BF16)] * 2,
        scratch_shapes=[pltpu.VMEM((tile + 2 * HALO, cc), F32), pltpu.VMEM((tile + 2 * HALO, cc), F32)],
        compiler_params=_cparams("parallel"),
        name="local_mixers",
    )(bd, bd, bd, conv_w, conv_b, ln_g, ln_b, pool_w, pool_scale)


def _merge_kernel(ya_ref, yc_ref, yb_ref, yd_ref, gate_ref, x_ref, mod_ref, g_ref, wb_ref, wo_ref,
                  x1_ref, h2b_ref, h2f_ref):
    d = x_ref.shape[1]
    acc = None
    for n, y_ref in enumerate((ya_ref, yc_ref, yb_ref, yd_ref)):
        term = gate_ref[:, n * d:(n + 1) * d].astype(F32) * jnp.dot(y_ref[...], wb_ref[n],
                                                                   preferred_element_type=F32)
        acc = term if acc is None else acc + term
    out = jnp.dot(acc.astype(BF16), wo_ref[...], preferred_element_type=F32)
    x1 = x_ref[...] + mod_ref[0, 2:3, :] * out
    x1_ref[...] = x1
    h2 = _rms(x1) * g_ref[...] * (1.0 + mod_ref[0, 4:5, :]) + mod_ref[0, 3:4, :]
    h2f_ref[...] = h2
    h2b_ref[...] = h2.astype(BF16)


def merge_branches(geo, n_tokens, ya, yc, yb, yd, gates, x, mod, gain2, w_branch, w_out):
    n_all, D = x.shape
    tm = geo["tm_tok"]
    grp = geo["group_map"](tm)
    cc = BRANCH_W
    yspec = pl.BlockSpec((tm, cc), lambda i: (i, 0))
    row = pl.BlockSpec((tm, D), lambda i: (i, 0))
    return pl.pallas_call(
        _merge_kernel,
        grid=(n_tokens // tm,),
        in_specs=[yspec, yspec, yspec, yspec,
                  pl.BlockSpec((tm, N_BRANCH * D), lambda i: (i, 0)),
                  row,
                  pl.BlockSpec((1, 6, D), lambda i: (grp(i), 0, 0)),
                  pl.BlockSpec((1, D), lambda i: (0, 0)),
                  pl.BlockSpec((N_BRANCH, cc, D), lambda i: (0, 0, 0)),
                  pl.BlockSpec((D, D), lambda i: (0, 0))],
        out_specs=[row, row, row],
        out_shape=[jax.ShapeDtypeStruct((n_all, D), F32), jax.ShapeDtypeStruct((n_all, D), BF16),
                   jax.ShapeDtypeStruct((n_all, D), F32)],
        compiler_params=_cparams("parallel"),
        name="merge_branches",
    )(ya, yc, yb, yd, gates, x, mod, gain2, w_branch, w_out)


_CAND_PAIRS = [(a, b) for a in range(PEER_TOPK) for b in range(PEER_TOPK) if (a + 1) * (b + 1) <= PEER_TOPK]
_CAND_ROWS = -(-len(_CAND_PAIRS) // 8) * 8


def _extract_top(s, row, n_rows):
    m = jnp.max(s, axis=0, keepdims=True)
    am = jnp.min(jnp.where(s == m, row, n_rows), axis=0, keepdims=True)
    return m, am


def _route_kernel(h_ref, wq_ref, k1_ref, k2_ref, idx_ref, gate_ref,
                  v1_s, i1_s, v2_s, i2_s, cand_s, cidx_s, top_s, idxt_s, gatet_s):
    tm = h_ref.shape[0]
    q = jnp.dot(h_ref[...], wq_ref[...], preferred_element_type=F32).astype(BF16)
    nt = (((1,), (1,)), ((), ()))
    key_row = lax.broadcasted_iota(jnp.int32, (PEER_NKEYS, tm), 0)
    cand_row = lax.broadcasted_iota(jnp.int32, (_CAND_ROWS, tm), 0)
    if _CAND_ROWS > len(_CAND_PAIRS):
        cand_s[len(_CAND_PAIRS):_CAND_ROWS, :] = jnp.full((_CAND_ROWS - len(_CAND_PAIRS), tm), NEG_BIG, F32)
        cidx_s[len(_CAND_PAIRS):_CAND_ROWS, :] = jnp.zeros((_CAND_ROWS - len(_CAND_PAIRS), tm), jnp.int32)
    for h in range(PEER_HEADS):
        for c, (k_ref, v_s, i_s) in enumerate(((k1_ref, v1_s, i1_s), (k2_ref, v2_s, i2_s))):
            col = (h * 2 + c) * PEER_NKEYS
            s = lax.dot_general(k_ref[h], q[:, col:col + PEER_NKEYS], nt, preferred_element_type=F32)
            for r in range(PEER_TOPK):
                m, am = _extract_top(s, key_row, PEER_NKEYS)
                v_s[r:r + 1, :] = m
                i_s[r:r + 1, :] = am
                s = jnp.where(key_row == am, NEG_BIG, s)
        a_prev, off = -1, 0
        for a in range(PEER_TOPK):
            nb_a = PEER_TOPK // (a + 1)
            cand_s[off:off + nb_a, :] = v1_s[a:a + 1, :] + v2_s[0:nb_a, :]
            cidx_s[off:off + nb_a, :] = i1_s[a:a + 1, :] * PEER_NKEYS + i2_s[0:nb_a, :]
            off += nb_a
        cand = cand_s[...]
        cidx = cidx_s[...]
        for r in range(PEER_TOPK):
            m, am = _extract_top(cand, cand_row, _CAND_ROWS)
            hit = cand_row == am
            top_s[r:r + 1, :] = m
            idxt_s[h * PEER_TOPK + r:h * PEER_TOPK + r + 1, :] = jnp.sum(jnp.where(hit, cidx, 0), axis=0,
                                                                          keepdims=True)
            cand = jnp.where(hit, NEG_BIG, cand)
        top = top_s[...]
        e = jnp.exp(top - jnp.max(top, axis=0, keepdims=True))
        gatet_s[h * PEER_TOPK:(h + 1) * PEER_TOPK, :] = e / jnp.sum(e, axis=0, keepdims=True)
    idx_ref[...] = idxt_s[...].T
    gate_ref[...] = gatet_s[...].T


def peer_route(geo, n_tokens, h2, wq, k1, k2):
    n_all, D = h2.shape
    tm = _pick_tile((256, 128), geo["n_lat"], geo["n_ctx"])
    qw = wq.shape[1]
    kspec = pl.BlockSpec((PEER_HEADS, PEER_NKEYS, qw // (2 * PEER_HEADS)), lambda i: (0, 0, 0))
    out = pl.BlockSpec((tm, PEER_ROWS), lambda i: (i, 0))
    return pl.pallas_call(
        _route_kernel,
        grid=(n_tokens // tm,),
        in_specs=[pl.BlockSpec((tm, D), lambda i: (i, 0)),
                  pl.BlockSpec((D, qw), lambda i: (0, 0)),
                  kspec, kspec],
        out_specs=[out, out],
        out_shape=[jax.ShapeDtypeStruct((n_all, PEER_ROWS), jnp.int32),
                   jax.ShapeDtypeStruct((n_all, PEER_ROWS), F32)],
        scratch_shapes=[pltpu.VMEM((PEER_TOPK, tm), F32), pltpu.VMEM((PEER_TOPK, tm), jnp.int32),
                        pltpu.VMEM((PEER_TOPK, tm), F32), pltpu.VMEM((PEER_TOPK, tm), jnp.int32),
                        pltpu.VMEM((_CAND_ROWS, tm), F32), pltpu.VMEM((_CAND_ROWS, tm), jnp.int32),
                        pltpu.VMEM((PEER_TOPK, tm), F32),
                        pltpu.VMEM((PEER_ROWS, tm), jnp.int32), pltpu.VMEM((PEER_ROWS, tm), F32)],
        compiler_params=_cparams("parallel"),
        name="peer_route",
    )(h2, wq, k1, k2)


def _act_kernel(gate_ref, dots_ref, o_ref):
    d = dots_ref[...]
    o_ref[...] = gate_ref[...] * (0.5 * d * (1.0 + lax.erf(d * (2.0 ** -0.5))))


def peer_activation(geo, n_tokens, gate, dots):
    tm = geo["tm_mm"]
    spec = pl.BlockSpec((tm, PEER_ROWS), lambda i: (i, 0))
    return pl.pallas_call(
        _act_kernel,
        grid=(n_tokens // tm,),
        in_specs=[spec, spec],
        out_specs=spec,
        out_shape=jax.ShapeDtypeStruct((n_tokens, PEER_ROWS), F32),
        compiler_params=_cparams("parallel"),
        name="peer_activation",
    )(gate, dots)


def _rope_tables(seq_len, pad_rows, rope_cols):
    rows = seq_len // GRID_W
    row = jnp.repeat(jnp.arange(rows, dtype=F32), GRID_W)
    col = jnp.tile(jnp.arange(GRID_W, dtype=F32), rows)
    n_freq = HEAD_DIM // 4
    inv_freq = ROPE_THETA ** (-jnp.arange(n_freq, dtype=F32) / n_freq)
    ang = jnp.concatenate([row[:, None] * inv_freq, col[:, None] * inv_freq], axis=-1)
    cos_h = jnp.concatenate([jnp.cos(ang), jnp.cos(ang)], axis=-1)
    sin_h = jnp.concatenate([-jnp.sin(ang), jnp.sin(ang)], axis=-1)
    one, zero = jnp.ones_like(cos_h), jnp.zeros_like(sin_h)
    cos = jnp.concatenate([cos_h if r else one for r in rope_cols], axis=-1)
    sin = jnp.concatenate([sin_h if r else zero for r in rope_cols], axis=-1)
    w = cos.shape[1]
    cos = jnp.concatenate([cos, jnp.ones((pad_rows, w), F32)], axis=0)
    sin = jnp.concatenate([sin, jnp.zeros((pad_rows, w), F32)], axis=0)
    return cos, sin


def kernel(x, c, ctx, c_ctx, w_mod, b_mod, norm1_g, norm2_g, w_in, w_gate, b_gate, attn_sink, lam_q1, lam_k1, lam_q2, lam_k2, diff_norm_g, conv_w, conv_b, conv_ln_g, conv_ln_b, pool_w, pool_scale, w_branch, w_out, peer_wq, peer_k1, peer_k2, peer_u, peer_v, final_g):
    B, L, D = x.shape
    Lc = ctx.shape[1]
    depth = w_in.shape[0]
    assert L % GRID_W == 0 and L % BLOCK == 0 and Lc % BLOCK == 0 and L % Lc == 0

    n_streams = N_STREAMS if B % N_STREAMS == 0 else 1
    bs = B // n_streams
    n_lat, n_ctx = bs * L, bs * Lc
    n_all = n_lat + n_ctx

    tm_mm = _pick_tile((1024, 512, 256, 128), L, n_ctx)
    tm_tok = _pick_tile((512, 256, 128), L, n_ctx)
    geo = dict(
        B=bs, L=L, Lc=Lc, n_lat=n_lat, n_ctx=n_ctx, n_all=n_all, tm_mm=tm_mm, tm_tok=tm_tok,
        group_map=lambda tm: (lambda i: jnp.where(i < n_lat // tm, (i * tm) // L, bs)),
        pos_map=lambda tm: (lambda i: jnp.where(i < n_lat // tm, i % (L // tm), L // tm)),
    )

    n_mod_rows = -(-(bs + 1) // 8) * 8
    streams = []
    for s in range(n_streams):
        sl = slice(s * bs, (s + 1) * bs)
        streams.append(dict(
            xa=jnp.concatenate([x[sl].reshape(n_lat, D), ctx[sl].reshape(n_ctx, D)], axis=0),
            cvec=jnp.concatenate([c[sl], c_ctx[None, :], jnp.zeros((n_mod_rows - bs - 1, D), F32)], axis=0),
            ffn=None, mod_prev=None))

    head_a = [True] * WA_HEADS + [True] * WA_KV_HEADS + [False] * WA_KV_HEADS
    head_c = [True] * (2 * BRANCH_W // HEAD_DIM) + [False] * (BRANCH_W // HEAD_DIM)
    cos_a, sin_a = _rope_tables(L, tm_mm, head_a)
    cos_c, sin_c = _rope_tables(L, tm_mm, head_c)

    for i in range(depth):
        ctx_out = i < depth - 1
        n_tok = n_all if ctx_out else n_lat
        lam_init = 0.8 - 0.6 * math.exp(-0.3 * i)
        w_in_b = w_in[i].astype(BF16)
        w_a, w_c, w_bd = (w_in_b[:, :A_COLS], w_in_b[:, A_COLS:A_COLS + C_COLS], w_in_b[:, A_COLS + C_COLS:])
        w_g = jnp.transpose(w_gate[i], (1, 0, 2)).reshape(D, N_BRANCH * D).astype(BF16)
        b_g = b_gate[i].reshape(1, N_BRANCH * D)
        w_br, w_o = w_branch[i].astype(BF16), w_out[i].astype(BF16)
        w_q, k1, k2 = peer_wq[i].astype(BF16), peer_k1[i].astype(BF16), peer_k2[i].astype(BF16)
        pool_w_b = pool_w[i].astype(BF16)
        lam_params = jnp.stack([lam_q1[i], lam_k1[i], lam_q2[i], lam_k2[i]], axis=0)

        for st in streams:
            mod = mod_vectors(st["cvec"], w_mod[i], b_mod[i]).reshape(n_mod_rows, 6, D)
            if st["ffn"] is None:
                xa = st["xa"]
                (h,) = pre_norm(geo, n_all, xa, None, None, mod, norm1_g[i][None, :])
            else:
                xa, h = pre_norm(geo, n_all, st["xa"], st["ffn"], st["mod_prev"], mod, norm1_g[i][None, :])
            sa = token_matmul(geo, n_all, h, w_a, "rope", (cos_a, sin_a))
            sc = token_matmul(geo, n_all, h, w_c, "rope", (cos_c, sin_c))
            ya = window_attention(geo, sa, attn_sink[i], ctx_out)
            yc = diff_attention(geo, sc, lam_params, diff_norm_g[i][None, :], lam_init, ctx_out)
            bd = token_matmul(geo, n_tok, h, w_bd)
            yb, yd = local_mixers(geo, n_tok, bd, conv_w[i], conv_b[i][None, :], conv_ln_g[i][None, :],
                                  conv_ln_b[i][None, :], pool_w_b, pool_scale[i][None, :])
            gates = token_matmul(geo, n_tok, h, w_g, "sigmoid", (b_g,))
            xa, h2b, h2f = merge_branches(geo, n_tok, ya, yc, yb, yd, gates, xa, mod, norm2_g[i][None, :],
                                          w_br, w_o)
            idx, gate = peer_route(geo, n_tok, h2b, w_q, k1, k2)
            dots = sc_peer_dots(h2f, idx, peer_u[i], n_tok)
            wgt = peer_activation(geo, n_tok, gate, dots)
            st.update(xa=xa, ffn=sc_peer_wsum(wgt, idx, peer_v[i], n_tok), mod_prev=mod)

    outs = [pre_norm(geo, n_lat, st["xa"], st["ffn"], st["mod_prev"], st["mod_prev"], final_g[None, :],
                     final=True)[0].reshape(bs, L, D) for st in streams]
    return jnp.concatenate(outs, axis=0)
```

```python
import functools
import math

import jax
import jax.numpy as jnp
from jax import lax
from jax.experimental import pallas as pl
from jax.experimental.pallas import tpu as pltpu
from jax.experimental.pallas import tpu_sc as plsc

GRID_W = 64
HEAD_DIM = 64
ROPE_THETA = 10000.0
BLOCK = 128
WINDOW = 128
N_BRANCH = 4
WA_HEADS = 8
WA_KV_HEADS = 2
WA_GROUP = WA_HEADS // WA_KV_HEADS
DIFF_HEADS = 4
CONV_K = 31
POOL_WINDOWS = (2, 4, 8, 16)
PEER_HEADS = 8
PEER_NKEYS = 128
PEER_TOPK = 16
EPS = 1e-6
NEG_INF = -1e30
NEG_BIG = -3.0e38
BRANCH_W = WA_HEADS * HEAD_DIM
A_COLS = BRANCH_W + 2 * WA_KV_HEADS * HEAD_DIM
C_COLS = 3 * BRANCH_W
BD_COLS = 3 * BRANCH_W

LANES = 128
VMEM_LIMIT = 48 * 1024 * 1024
SC_LANES = 16
SC_WORKERS = 32
PEER_ROWS = PEER_HEADS * PEER_TOPK
GATHER_ROWS = 32
GATHER_BUFFERS = 4
BF16_SUM = 4
SC_TOKEN_BLOCK = 16
N_STREAMS = 2

F32 = jnp.float32
BF16 = jnp.bfloat16


def _cparams(*sem):
    return pltpu.CompilerParams(dimension_semantics=sem, vmem_limit_bytes=VMEM_LIMIT)


def _pick_tile(candidates, *sizes):
    for t in candidates:
        if all(s % t == 0 for s in sizes):
            return t
    raise ValueError(f"no tile in {candidates} divides {sizes}")


def _rms(x):
    return x * lax.rsqrt(jnp.mean(x * x, axis=-1, keepdims=True) + EPS)


def _pack_bf16_pairs(bits):
    rounded = bits + 0x7FFF + (lax.shift_right_logical(bits, 16) & 1)
    half = bits.shape[1] // 2
    return (rounded[:, half:] & jnp.int32(-65536)) | lax.shift_right_logical(rounded[:, :half], 16)


def _sc_mesh():
    return plsc.VectorSubcoreMesh(core_axis_name="c", subcore_axis_name="s")


def _sc_worker_id():
    return lax.axis_index("s") * 2 + lax.axis_index("c")


def _lane_broadcast(vec, lane, r):
    return jnp.full((SC_LANES,), jnp.sum(jnp.where(lane == r, vec, 0.0)), F32)


def _tree_sum(parts):
    while len(parts) > 1:
        parts = [parts[i] + parts[i + 1] for i in range(0, len(parts), 2)]
    return parts[0]


def _sc_peer_call(kind, a, idx, table, n_tokens):
    T = n_tokens
    DP = table.shape[1]
    D = 2 * DP
    TB, GR, NBUF = SC_TOKEN_BLOCK, GATHER_ROWS, GATHER_BUFFERS
    assert T % (SC_WORKERS * TB) == 0 and DP % (SC_LANES * BF16_SUM) == 0 and idx.shape[1] == PEER_ROWS
    tpw = T // SC_WORKERS
    n_chunks = DP // SC_LANES
    n_groups = PEER_ROWS // GR
    n_steps = TB * n_groups
    a_width = a.shape[1]
    out_width = PEER_ROWS if kind == "dots" else D

    def body(a_hbm, idx_hbm, tab_hbm, out_hbm, idx_v, a_v, rows_v, out_v, sems):
        wid = _sc_worker_id()
        lane = lax.iota(jnp.int32, SC_LANES)

        def gather(step):
            t, g, slot = step // n_groups, step % n_groups, step % NBUF
            return pltpu.make_async_copy(tab_hbm.at[idx_v.at[t, pl.ds(g * GR, GR)]],
                                         rows_v.at[pl.ds(slot * GR, GR)], sems.at[slot])

        def packed(ref, r, c):
            return plsc.bitcast(ref[r, pl.ds(c * SC_LANES, SC_LANES)], BF16)

        def unpack_f32(p):
            return plsc.unpack(p, format=plsc.PackFormat.INTERLEAVED, preferred_element_type=F32)

        def dots_group(t, g, base):
            xs = [packed(a_v, t, k) for k in range(n_chunks)]
            for half in range(GR // SC_LANES):
                r0 = base + half * SC_LANES

                def row(r, tot):
                    parts = []
                    for k0 in range(0, n_chunks, BF16_SUM):
                        prods = [xs[k0 + k] * packed(rows_v, r0 + r, k0 + k) for k in range(BF16_SUM)]
                        lo, hi = unpack_f32(_tree_sum(prods))
                        parts.append(lo + hi)
                    return jnp.where(lane == r, jnp.sum(_tree_sum(parts)), tot)

                tot = plsc.parallel_loop(0, SC_LANES, 1, carry=jnp.zeros((SC_LANES,), F32))(row)
                out_v[t, pl.ds(g * GR + half * SC_LANES, SC_LANES)] = tot

        def wsum_group(t, g, base):
            for half in range(GR // SC_LANES):
                r0 = base + half * SC_LANES
                wvec = a_v[t, pl.ds(g * GR + half * SC_LANES, SC_LANES)]
                wb = []
                for r in range(SC_LANES):
                    s = _lane_broadcast(wvec, lane, r)
                    wb.append(plsc.pack(s, s, format=plsc.PackFormat.INTERLEAVED, preferred_element_type=BF16))
                first = (g == 0 and half == 0)

                @plsc.parallel_loop(0, n_chunks, 1, unroll=2)
                def _(c):
                    lo_parts, hi_parts = [], []
                    for r in range(0, SC_LANES, 2):
                        lo, hi = unpack_f32(wb[r] * packed(rows_v, r0 + r, c)
                                            + wb[r + 1] * packed(rows_v, r0 + r + 1, c))
                        lo_parts.append(lo)
                        hi_parts.append(hi)
                    for sl, s in ((pl.ds(c * SC_LANES, SC_LANES), _tree_sum(lo_parts)),
                                  (pl.ds(D // 2 + c * SC_LANES, SC_LANES), _tree_sum(hi_parts))):
                        out_v[t, sl] = s if first else out_v[t, sl] + s

        group = dots_group if kind == "dots" else wsum_group

        @pl.loop(0, tpw // TB)
        def _(b):
            tok0 = wid * tpw + b * TB
            pltpu.sync_copy(idx_hbm.at[pl.ds(tok0, TB)], idx_v)
            pltpu.sync_copy(a_hbm.at[pl.ds(tok0, TB)], a_v)
            for p in range(NBUF - 1):
                gather(p).start()

            @pl.loop(0, TB)
            def _(t):
                for g in range(n_groups):
                    step = t * n_groups + g

                    @pl.when(step + NBUF - 1 < n_steps)
                    def _():
                        gather(step + NBUF - 1).start()

                    gather(step).wait()
                    group(t, g, (step % NBUF) * GR)

            pltpu.sync_copy(out_v, out_hbm.at[pl.ds(tok0, TB)])

    f = pl.kernel(
        body,
        out_type=jax.ShapeDtypeStruct((T, out_width), F32),
        mesh=_sc_mesh(),
        compiler_params=pltpu.CompilerParams(needs_layout_passes=False),
        scratch_types=[
            pltpu.VMEM((TB, PEER_ROWS), jnp.int32),
            pltpu.VMEM((TB, a_width), a.dtype),
            pltpu.VMEM((NBUF * GR, DP), jnp.int32),
            pltpu.VMEM((TB, out_width), F32),
            pltpu.SemaphoreType.DMA((NBUF,)),
        ],
    )
    return f(a, idx, table)


def sc_peer_dots(x, idx, table, n_tokens):
    return _sc_peer_call("dots", x, idx, table, n_tokens)


def sc_peer_wsum(w, idx, table, n_tokens):
    return _sc_peer_call("wsum", w, idx, table, n_tokens)


def _mod_kernel(c_ref, w_ref, b_ref, o_ref):
    cv = c_ref[...]
    s = cv * jax.nn.sigmoid(cv)
    o_ref[...] = jnp.dot(s, w_ref[...], precision=lax.Precision.HIGHEST,
                         preferred_element_type=F32) + b_ref[...]


def mod_vectors(cvec, w_mod, b_mod):
    R, D = cvec.shape
    W = w_mod.shape[1]
    tn = _pick_tile((512, 256, 128), W)
    return pl.pallas_call(
        _mod_kernel,
        grid=(W // tn,),
        in_specs=[pl.BlockSpec((R, D), lambda j: (0, 0)),
                  pl.BlockSpec((D, tn), lambda j: (0, j)),
                  pl.BlockSpec((1, tn), lambda j: (0, j))],
        out_specs=pl.BlockSpec((R, tn), lambda j: (0, j)),
        out_shape=jax.ShapeDtypeStruct((R, W), F32),
        compiler_params=_cparams("arbitrary"),
        name="mod_vectors",
    )(cvec, w_mod, b_mod.reshape(1, W))


def _pre_kernel(has_ffn, final, *refs):
    if has_ffn:
        x_ref, ffn_ref, modp_ref, mod_ref, g_ref = refs[:5]
        outs = refs[5:]
        x = x_ref[...] + modp_ref[0, 5:6, :] * ffn_ref[...]
    else:
        x_ref, mod_ref, g_ref = refs[:3]
        outs = refs[3:]
        x = x_ref[...]
    y = _rms(x) * g_ref[...]
    if final:
        outs[0][...] = y
        return
    h = y * (1.0 + mod_ref[0, 1:2, :]) + mod_ref[0, 0:1, :]
    if has_ffn:
        outs[0][...] = x
        outs[1][...] = h.astype(BF16)
    else:
        outs[0][...] = h.astype(BF16)


def pre_norm(geo, n_tokens, x, ffn, mod_prev, mod, gain, final=False):
    D = x.shape[1]
    tm = geo["tm_tok"]
    grp = geo["group_map"](tm)
    has_ffn = ffn is not None
    row = pl.BlockSpec((tm, D), lambda i: (i, 0))
    modspec = pl.BlockSpec((1, 6, D), lambda i: (grp(i), 0, 0))
    gspec = pl.BlockSpec((1, D), lambda i: (0, 0))
    if has_ffn:
        args = (x, ffn, mod_prev, mod, gain)
        in_specs = [row, row, modspec, modspec, gspec]
    else:
        args = (x, mod, gain)
        in_specs = [row, modspec, gspec]
    n_all = x.shape[0]
    if final:
        out_shape = [jax.ShapeDtypeStruct((n_tokens, D), F32)]
    elif has_ffn:
        out_shape = [jax.ShapeDtypeStruct((n_all, D), F32), jax.ShapeDtypeStruct((n_all, D), BF16)]
    else:
        out_shape = [jax.ShapeDtypeStruct((n_all, D), BF16)]
    outs = pl.pallas_call(
        functools.partial(_pre_kernel, has_ffn, final),
        grid=(n_tokens // tm,),
        in_specs=in_specs,
        out_specs=[row] * len(out_shape),
        out_shape=out_shape,
        compiler_params=_cparams("parallel"),
        name="pre_norm",
    )(*args)
    return outs


def _mm_kernel(epi, tn, a_ref, w_ref, *rest):
    acc = jnp.dot(a_ref[...], w_ref[...], preferred_element_type=F32)
    if epi == "rope":
        cos_ref, sin_ref, o_ref = rest
        for ch in range(tn // LANES):
            sl = slice(ch * LANES, (ch + 1) * LANES)
            a = acc[:, sl]
            lane = lax.broadcasted_iota(jnp.int32, a.shape, 1)
            partner = jnp.where((lane % HEAD_DIM) < HEAD_DIM // 2,
                                pltpu.roll(a, LANES - HEAD_DIM // 2, 1), pltpu.roll(a, HEAD_DIM // 2, 1))
            o_ref[:, sl] = (a * cos_ref[:, sl] + partner * sin_ref[:, sl]).astype(o_ref.dtype)
    elif epi == "sigmoid":
        b_ref, o_ref = rest
        o_ref[...] = jax.nn.sigmoid(acc + b_ref[...]).astype(o_ref.dtype)
    else:
        (o_ref,) = rest
        o_ref[...] = acc.astype(o_ref.dtype)


def token_matmul(geo, n_tokens, a, w, epi="none", extras=()):
    n_all, K = a.shape
    W = w.shape[1]
    tm = geo["tm_mm"]
    tn = _pick_tile((768, 512, 256, 128), W)
    in_specs = [pl.BlockSpec((tm, K), lambda i, j: (i, 0)), pl.BlockSpec((K, tn), lambda i, j: (0, j))]
    if epi == "rope":
        pos = geo["pos_map"](tm)
        in_specs += [pl.BlockSpec((tm, tn), lambda i, j: (pos(i), j))] * 2
    elif epi == "sigmoid":
        in_specs += [pl.BlockSpec((1, tn), lambda i, j: (0, j))]
    return pl.pallas_call(
        functools.partial(_mm_kernel, epi, tn),
        grid=(n_tokens // tm, W // tn),
        in_specs=in_specs,
        out_specs=pl.BlockSpec((tm, tn), lambda i, j: (i, j)),
        out_shape=jax.ShapeDtypeStruct((n_all, W), BF16),
        compiler_params=_cparams("parallel", "arbitrary"),
        name="token_matmul_" + epi,
    )(a, w, *extras)


def _attn_a_kernel(nb, seq_len, sink_ref, cur_ref, prev_ref, next_ref, ctx_ref, o_ref):
    n = pl.program_id(1)
    is_lat = n < nb
    qi = lax.broadcasted_iota(jnp.int32, (BLOCK, 3 * BLOCK), 0)
    kj = lax.broadcasted_iota(jnp.int32, (BLOCK, 3 * BLOCK), 1)
    kabs = (n - 1) * BLOCK + kj
    band = (jnp.abs(kj - BLOCK - qi) <= WINDOW) & (kabs >= 0) & (kabs < seq_len) & is_lat
    nt = (((1,), (1,)), ((), ()))
    kv_off = WA_HEADS * HEAD_DIM
    for kvh in range(WA_KV_HEADS):
        kc = slice(kv_off + kvh * HEAD_DIM, kv_off + (kvh + 1) * HEAD_DIM)
        vc = slice(kv_off + (WA_KV_HEADS + kvh) * HEAD_DIM, kv_off + (WA_KV_HEADS + kvh + 1) * HEAD_DIM)
        k_lat = jnp.concatenate([prev_ref[:, kc], cur_ref[:, kc], next_ref[:, kc]], axis=0)
        v_lat = jnp.concatenate([prev_ref[:, vc], cur_ref[:, vc], next_ref[:, vc]], axis=0)
        k_ctx = ctx_ref[:, kc]
        v_ctx = ctx_ref[:, vc]
        for g in range(WA_GROUP):
            h = kvh * WA_GROUP + g
            qh = cur_ref[:, h * HEAD_DIM:(h + 1) * HEAD_DIM]
            s1 = lax.dot_general(qh, k_lat, nt, preferred_element_type=F32) * (HEAD_DIM ** -0.5)
            s1 = jnp.where(band, s1, NEG_INF)
            s2 = lax.dot_general(qh, k_ctx, nt, preferred_element_type=F32) * (HEAD_DIM ** -0.5)
            sink = sink_ref[h]
            m = jnp.maximum(jnp.maximum(jnp.max(s1, axis=1, keepdims=True),
                                        jnp.max(s2, axis=1, keepdims=True)), sink)
            e1 = jnp.exp(s1 - m)
            e2 = jnp.exp(s2 - m)
            den = jnp.sum(e1, axis=1, keepdims=True) + jnp.sum(e2, axis=1, keepdims=True) + jnp.exp(sink - m)
            pv = (jnp.dot(e1.astype(BF16), v_lat, preferred_element_type=F32)
                  + jnp.dot(e2.astype(BF16), v_ctx, preferred_element_type=F32))
            o_ref[:, h * HEAD_DIM:(h + 1) * HEAD_DIM] = (pv / den).astype(o_ref.dtype)


def window_attention(geo, sa, sink, ctx_out):
    B, L, Lc, n_lat, n_all = geo["B"], geo["L"], geo["Lc"], geo["n_lat"], geo["n_all"]
    nb, ncb = L // BLOCK, Lc // BLOCK
    lat0, ctx0 = n_lat // BLOCK, n_lat // Lc

    def cur(b, n):
        return jnp.where(n < nb, b * nb + n, lat0 + b * ncb + (n - nb))

    def prev(b, n):
        return jnp.where(n < nb, b * nb + jnp.maximum(n - 1, 0), lat0 + b * ncb + (n - nb))

    def nxt(b, n):
        return jnp.where(n < nb, b * nb + jnp.minimum(n + 1, nb - 1), lat0 + b * ncb + (n - nb))

    blk = (BLOCK, A_COLS)
    return pl.pallas_call(
        functools.partial(_attn_a_kernel, nb, L),
        grid=(B, nb + (ncb if ctx_out else 0)),
        in_specs=[pl.BlockSpec(memory_space=pltpu.SMEM),
                  pl.BlockSpec(blk, lambda b, n: (cur(b, n), 0)),
                  pl.BlockSpec(blk, lambda b, n: (prev(b, n), 0)),
                  pl.BlockSpec(blk, lambda b, n: (nxt(b, n), 0)),
                  pl.BlockSpec((Lc, A_COLS), lambda b, n: (ctx0 + b, 0))],
        out_specs=pl.BlockSpec((BLOCK, BRANCH_W), lambda b, n: (cur(b, n), 0)),
        out_shape=jax.ShapeDtypeStruct((n_all, BRANCH_W), BF16),
        compiler_params=_cparams("parallel", "arbitrary"),
        name="window_attention",
    )(sink, sa, sa, sa, sa)


def _attn_c_kernel(seq_len, ctx_len, tq, lam_init, lat_ref, ctx_ref, lam_ref, g_ref, o_ref, kall, vall):
    n = pl.program_id(1)
    nlq = seq_len // tq
    kcol = slice(BRANCH_W, 2 * BRANCH_W)
    vcol = slice(2 * BRANCH_W, 3 * BRANCH_W)

    @pl.when(n == 0)
    def _():
        kall[0:seq_len, :] = lat_ref[:, kcol]
        kall[seq_len:seq_len + ctx_len, :] = ctx_ref[:, kcol]
        vall[0:seq_len, :] = lat_ref[:, vcol]
        vall[seq_len:seq_len + ctx_len, :] = ctx_ref[:, vcol]

    lam = (jnp.exp(jnp.sum(lam_ref[0:1, :] * lam_ref[1:2, :], axis=1, keepdims=True))
           - jnp.exp(jnp.sum(lam_ref[2:3, :] * lam_ref[3:4, :], axis=1, keepdims=True)) + lam_init)
    nt = (((1,), (1,)), ((), ()))

    def heads(q, k0, klen):
        for h in range(DIFF_HEADS):
            probs = []
            for c in range(2):
                col = slice(h * 2 * HEAD_DIM + c * HEAD_DIM, h * 2 * HEAD_DIM + (c + 1) * HEAD_DIM)
                s = lax.dot_general(q[:, col], kall[k0:k0 + klen, col], nt,
                                    preferred_element_type=F32) * (HEAD_DIM ** -0.5)
                e = jnp.exp(s - jnp.max(s, axis=1, keepdims=True))
                probs.append(e * (1.0 / jnp.sum(e, axis=1, keepdims=True)))
            a = probs[0] - lam * probs[1]
            hv = slice(h * 2 * HEAD_DIM, (h + 1) * 2 * HEAD_DIM)
            o = jnp.dot(a.astype(BF16), vall[k0:k0 + klen, hv], preferred_element_type=F32)
            o_ref[:, hv] = (_rms(o) * g_ref[...] * (1.0 - lam_init)).astype(o_ref.dtype)

    @pl.when(n < nlq)
    def _():
        heads(lat_ref[pl.ds(pl.multiple_of(n * tq, tq), tq), 0:BRANCH_W], 0, seq_len + ctx_len)

    @pl.when(n >= nlq)
    def _():
        heads(ctx_ref[pl.ds(pl.multiple_of((n - nlq) * tq, tq), tq), 0:BRANCH_W], seq_len, ctx_len)


def diff_attention(geo, sc, lam_params, diff_g, lam_init, ctx_out):
    B, L, Lc, n_lat, n_all = geo["B"], geo["L"], geo["Lc"], geo["n_lat"], geo["n_all"]
    tq = _pick_tile((256, 128), L, Lc)
    nlq, ncq = L // tq, Lc // tq
    lat0, ctx0 = n_lat // tq, n_lat // Lc

    def orow(b, n):
        return jnp.where(n < nlq, b * nlq + n, lat0 + b * ncq + (n - nlq))

    return pl.pallas_call(
        functools.partial(_attn_c_kernel, L, Lc, tq, lam_init),
        grid=(B, nlq + (ncq if ctx_out else 0)),
        in_specs=[pl.BlockSpec((L, C_COLS), lambda b, n: (b, 0)),
                  pl.BlockSpec((Lc, C_COLS), lambda b, n: (ctx0 + b, 0)),
                  pl.BlockSpec((4, HEAD_DIM), lambda b, n: (0, 0)),
                  pl.BlockSpec((1, 2 * HEAD_DIM), lambda b, n: (0, 0))],
        out_specs=pl.BlockSpec((tq, BRANCH_W), lambda b, n: (orow(b, n), 0)),
        out_shape=jax.ShapeDtypeStruct((n_all, BRANCH_W), BF16),
        scratch_shapes=[pltpu.VMEM((L + Lc, BRANCH_W), BF16), pltpu.VMEM((L + Lc, BRANCH_W), BF16)],
        compiler_params=_cparams("parallel", "arbitrary"),
        name="diff_attention",
    )(sc, sc, lam_params, diff_g)


HALO = 16
CONV_ROWS = 64


def _local_kernel(tile, seq_len, ctx_len, n_lat_tiles, cur_ref, prev_ref, next_ref, cw_ref, cb_ref,
                  lg_ref, lb_ref, pw_ref, ps_ref, yb_ref, yd_ref, ypad, xpad):
    i = pl.program_id(0)
    tiles_per_seq = seq_len // tile
    is_lat = i < n_lat_tiles
    pos = jnp.where(is_lat, i % tiles_per_seq, 0)
    prev_ok = is_lat & (pos > 0)
    next_ok = is_lat & (pos < tiles_per_seq - 1)
    cc = BRANCH_W

    def glu(ref):
        return ref[:, 0:cc].astype(F32) * jax.nn.sigmoid(ref[:, cc:2 * cc].astype(F32))

    def pool_in(ref):
        return ref[:, 2 * cc:3 * cc].astype(F32)

    ypad[0:HALO, :] = jnp.where(prev_ok, glu(prev_ref), 0.0)
    ypad[HALO:HALO + tile, :] = glu(cur_ref)
    ypad[HALO + tile:2 * HALO + tile, :] = jnp.where(next_ok, glu(next_ref), 0.0)
    xpad[0:HALO, :] = jnp.where(prev_ok, pool_in(prev_ref), 0.0)
    xpad[HALO:HALO + tile, :] = pool_in(cur_ref)
    xpad[HALO + tile:2 * HALO + tile, :] = jnp.where(next_ok, pool_in(next_ref), 0.0)

    for r0 in range(0, tile, CONV_ROWS):
        acc = jnp.zeros((CONV_ROWS, cc), F32)
        for k in range(CONV_K):
            start = r0 + HALO - CONV_K // 2 + k
            acc = acc + cw_ref[k:k + 1, :] * ypad[start:start + CONV_ROWS, :]
        y = acc + cb_ref[...]
        mu = jnp.mean(y, axis=-1, keepdims=True)
        yc = y - mu
        var = jnp.mean(yc * yc, axis=-1, keepdims=True)
        z = yc * lax.rsqrt(var + EPS) * lg_ref[...] + lb_ref[...]
        yb_ref[r0:r0 + CONV_ROWS, :] = (z * jax.nn.sigmoid(z)).astype(yb_ref.dtype)

    gc = cc // len(POOL_WINDOWS)
    tpos = pos * tile + lax.broadcasted_iota(jnp.int32, (tile, 1), 0)
    last = jnp.where(is_lat, seq_len, ctx_len) - 1
    for g, win in enumerate(POOL_WINDOWS):
        lo, hi = win // 2, win - 1 - win // 2
        cols = slice(g * gc, (g + 1) * gc)
        wsum = jnp.zeros((tile, gc), F32)
        for j in range(-lo, hi + 1):
            wsum = wsum + xpad[HALO + j:HALO + j + tile, cols]
        cnt = (jnp.minimum(tpos + hi, last) - jnp.maximum(tpos - lo, 0) + 1).astype(F32)
        y = wsum / cnt - xpad[HALO:HALO + tile, cols]
        proj = jnp.dot(y.astype(BF16), pw_ref[g], preferred_element_type=F32)
        yd_ref[:, cols] = (proj * ps_ref[:, cols]).astype(yd_ref.dtype)


def local_mixers(geo, n_tokens, bd, conv_w, conv_b, ln_g, ln_b, pool_w, pool_scale):
    L, Lc, n_lat, n_all = geo["L"], geo["Lc"], geo["n_lat"], geo["n_all"]
    tile = Lc
    assert L % tile == 0 and tile % CONV_ROWS == 0 and tile % HALO == 0
    hb = tile // HALO
    n_halo_blocks = n_all // HALO
    cc = BRANCH_W
    vec = pl.BlockSpec((1, cc), lambda i: (0, 0))
    out = pl.BlockSpec((tile, cc), lambda i: (i, 0))
    return pl.pallas_call(
        functools.partial(_local_kernel, tile, L, Lc, n_lat // tile),
        grid=(n_tokens // tile,),
        in_specs=[pl.BlockSpec((tile, BD_COLS), lambda i: (i, 0)),
                  pl.BlockSpec((HALO, BD_COLS), lambda i: (jnp.maximum(i * hb - 1, 0), 0)),
                  pl.BlockSpec((HALO, BD_COLS), lambda i: (jnp.minimum((i + 1) * hb, n_halo_blocks - 1), 0)),
                  pl.BlockSpec((CONV_K, cc), lambda i: (0, 0)),
                  vec, vec, vec,
                  pl.BlockSpec((len(POOL_WINDOWS), cc // len(POOL_WINDOWS), cc // len(POOL_WINDOWS)),
                               lambda i: (0, 0, 0)),
                  vec],
        out_specs=[out, out],
        out_shape=[jax.ShapeDtypeStruct((n_all, cc), BF16)] * 2,
        scratch_shapes=[pltpu.VMEM((tile + 2 * HALO, cc), F32), pltpu.VMEM((tile + 2 * HALO, cc), F32)],
        compiler_params=_cparams("parallel"),
        name="local_mixers",
    )(bd, bd, bd, conv_w, conv_b, ln_g, ln_b, pool_w, pool_scale)


def _merge_kernel(ya_ref, yc_ref, yb_ref, yd_ref, gate_ref, x_ref, mod_ref, g_ref, wb_ref, wo_ref,
                  x1_ref, h2b_ref, h2p_ref):
    d = x_ref.shape[1]
    acc = None
    for n, y_ref in enumerate((ya_ref, yc_ref, yb_ref, yd_ref)):
        term = gate_ref[:, n * d:(n + 1) * d].astype(F32) * jnp.dot(y_ref[...], wb_ref[n],
                                                                   preferred_element_type=F32)
        acc = term if acc is None else acc + term
    out = jnp.dot(acc.astype(BF16), wo_ref[...], preferred_element_type=F32)
    x1 = x_ref[...] + mod_ref[0, 2:3, :] * out
    x1_ref[...] = x1
    h2 = _rms(x1) * g_ref[...] * (1.0 + mod_ref[0, 4:5, :]) + mod_ref[0, 3:4, :]
    h2b = h2.astype(BF16)
    h2b_ref[...] = h2b
    h2p_ref[...] = _pack_bf16_pairs(pltpu.bitcast(h2, jnp.int32))


def merge_branches(geo, n_tokens, ya, yc, yb, yd, gates, x, mod, gain2, w_branch, w_out):
    n_all, D = x.shape
    tm = geo["tm_tok"]
    grp = geo["group_map"](tm)
    cc = BRANCH_W
    yspec = pl.BlockSpec((tm, cc), lambda i: (i, 0))
    row = pl.BlockSpec((tm, D), lambda i: (i, 0))
    return pl.pallas_call(
        _merge_kernel,
        grid=(n_tokens // tm,),
        in_specs=[yspec, yspec, yspec, yspec,
                  pl.BlockSpec((tm, N_BRANCH * D), lambda i: (i, 0)),
                  row,
                  pl.BlockSpec((1, 6, D), lambda i: (grp(i), 0, 0)),
                  pl.BlockSpec((1, D), lambda i: (0, 0)),
                  pl.BlockSpec((N_BRANCH, cc, D), lambda i: (0, 0, 0)),
                  pl.BlockSpec((D, D), lambda i: (0, 0))],
        out_specs=[row, row, pl.BlockSpec((tm, D // 2), lambda i: (i, 0))],
        out_shape=[jax.ShapeDtypeStruct((n_all, D), F32), jax.ShapeDtypeStruct((n_all, D), BF16),
                   jax.ShapeDtypeStruct((n_all, D // 2), jnp.int32)],
        compiler_params=_cparams("parallel"),
        name="merge_branches",
    )(ya, yc, yb, yd, gates, x, mod, gain2, w_branch, w_out)


_CAND_PAIRS = [(a, b) for a in range(PEER_TOPK) for b in range(PEER_TOPK) if (a + 1) * (b + 1) <= PEER_TOPK]
_CAND_ROWS = -(-len(_CAND_PAIRS) // 8) * 8


def _extract_top(s, row, n_rows):
    m = jnp.max(s, axis=0, keepdims=True)
    am = jnp.min(jnp.where(s == m, row, n_rows), axis=0, keepdims=True)
    return m, am


def _route_kernel(h_ref, wq_ref, k1_ref, k2_ref, idx_ref, gate_ref,
                  v1_s, i1_s, v2_s, i2_s, cand_s, cidx_s, top_s, idxt_s, gatet_s):
    tm = h_ref.shape[0]
    q = jnp.dot(h_ref[...], wq_ref[...], preferred_element_type=F32).astype(BF16)
    nt = (((1,), (1,)), ((), ()))
    key_row = lax.broadcasted_iota(jnp.int32, (PEER_NKEYS, tm), 0)
    cand_row = lax.broadcasted_iota(jnp.int32, (_CAND_ROWS, tm), 0)
    if _CAND_ROWS > len(_CAND_PAIRS):
        cand_s[len(_CAND_PAIRS):_CAND_ROWS, :] = jnp.full((_CAND_ROWS - len(_CAND_PAIRS), tm), NEG_BIG, F32)
        cidx_s[len(_CAND_PAIRS):_CAND_ROWS, :] = jnp.zeros((_CAND_ROWS - len(_CAND_PAIRS), tm), jnp.int32)
    for h in range(PEER_HEADS):
        for c, (k_ref, v_s, i_s) in enumerate(((k1_ref, v1_s, i1_s), (k2_ref, v2_s, i2_s))):
            col = (h * 2 + c) * PEER_NKEYS
            s = lax.dot_general(k_ref[h], q[:, col:col + PEER_NKEYS], nt, preferred_element_type=F32)
            for r in range(PEER_TOPK):
                m, am = _extract_top(s, key_row, PEER_NKEYS)
                v_s[r:r + 1, :] = m
                i_s[r:r + 1, :] = am
                s = jnp.where(key_row == am, NEG_BIG, s)
        a_prev, off = -1, 0
        for a in range(PEER_TOPK):
            nb_a = PEER_TOPK // (a + 1)
            cand_s[off:off + nb_a, :] = v1_s[a:a + 1, :] + v2_s[0:nb_a, :]
            cidx_s[off:off + nb_a, :] = i1_s[a:a + 1, :] * PEER_NKEYS + i2_s[0:nb_a, :]
            off += nb_a
        cand = cand_s[...]
        cidx = cidx_s[...]
        for r in range(PEER_TOPK):
            m, am = _extract_top(cand, cand_row, _CAND_ROWS)
            hit = cand_row == am
            top_s[r:r + 1, :] = m
            idxt_s[h * PEER_TOPK + r:h * PEER_TOPK + r + 1, :] = jnp.sum(jnp.where(hit, cidx, 0), axis=0,
                                                                          keepdims=True)
            cand = jnp.where(hit, NEG_BIG, cand)
        top = top_s[...]
        e = jnp.exp(top - jnp.max(top, axis=0, keepdims=True))
        gatet_s[h * PEER_TOPK:(h + 1) * PEER_TOPK, :] = e / jnp.sum(e, axis=0, keepdims=True)
    idx_ref[...] = idxt_s[...].T
    gate_ref[...] = gatet_s[...].T


def peer_route(geo, n_tokens, h2, wq, k1, k2):
    n_all, D = h2.shape
    tm = _pick_tile((256, 128), geo["n_lat"], geo["n_ctx"])
    qw = wq.shape[1]
    kspec = pl.BlockSpec((PEER_HEADS, PEER_NKEYS, qw // (2 * PEER_HEADS)), lambda i: (0, 0, 0))
    out = pl.BlockSpec((tm, PEER_ROWS), lambda i: (i, 0))
    return pl.pallas_call(
        _route_kernel,
        grid=(n_tokens // tm,),
        in_specs=[pl.BlockSpec((tm, D), lambda i: (i, 0)),
                  pl.BlockSpec((D, qw), lambda i: (0, 0)),
                  kspec, kspec],
        out_specs=[out, out],
        out_shape=[jax.ShapeDtypeStruct((n_all, PEER_ROWS), jnp.int32),
                   jax.ShapeDtypeStruct((n_all, PEER_ROWS), F32)],
        scratch_shapes=[pltpu.VMEM((PEER_TOPK, tm), F32), pltpu.VMEM((PEER_TOPK, tm), jnp.int32),
                        pltpu.VMEM((PEER_TOPK, tm), F32), pltpu.VMEM((PEER_TOPK, tm), jnp.int32),
                        pltpu.VMEM((_CAND_ROWS, tm), F32), pltpu.VMEM((_CAND_ROWS, tm), jnp.int32),
                        pltpu.VMEM((PEER_TOPK, tm), F32),
                        pltpu.VMEM((PEER_ROWS, tm), jnp.int32), pltpu.VMEM((PEER_ROWS, tm), F32)],
        compiler_params=_cparams("parallel"),
        name="peer_route",
    )(h2, wq, k1, k2)


def _act_kernel(gate_ref, dots_ref, o_ref):
    d = dots_ref[...]
    o_ref[...] = gate_ref[...] * (0.5 * d * (1.0 + lax.erf(d * (2.0 ** -0.5))))


def peer_activation(geo, n_tokens, gate, dots):
    tm = geo["tm_mm"]
    spec = pl.BlockSpec((tm, PEER_ROWS), lambda i: (i, 0))
    return pl.pallas_call(
        _act_kernel,
        grid=(n_tokens // tm,),
        in_specs=[spec, spec],
        out_specs=spec,
        out_shape=jax.ShapeDtypeStruct((n_tokens, PEER_ROWS), F32),
        compiler_params=_cparams("parallel"),
        name="peer_activation",
    )(gate, dots)


def _rope_tables(seq_len, pad_rows, rope_cols):
    rows = seq_len // GRID_W
    row = jnp.repeat(jnp.arange(rows, dtype=F32), GRID_W)
    col = jnp.tile(jnp.arange(GRID_W, dtype=F32), rows)
    n_freq = HEAD_DIM // 4
    inv_freq = ROPE_THETA ** (-jnp.arange(n_freq, dtype=F32) / n_freq)
    ang = jnp.concatenate([row[:, None] * inv_freq, col[:, None] * inv_freq], axis=-1)
    cos_h = jnp.concatenate([jnp.cos(ang), jnp.cos(ang)], axis=-1)
    sin_h = jnp.concatenate([-jnp.sin(ang), jnp.sin(ang)], axis=-1)
    one, zero = jnp.ones_like(cos_h), jnp.zeros_like(sin_h)
    cos = jnp.concatenate([cos_h if r else one for r in rope_cols], axis=-1)
    sin = jnp.concatenate([sin_h if r else zero for r in rope_cols], axis=-1)
    w = cos.shape[1]
    cos = jnp.concatenate([cos, jnp.ones((pad_rows, w), F32)], axis=0)
    sin = jnp.concatenate([sin, jnp.zeros((pad_rows, w), F32)], axis=0)
    return cos, sin


def kernel(x, c, ctx, c_ctx, w_mod, b_mod, norm1_g, norm2_g, w_in, w_gate, b_gate, attn_sink, lam_q1, lam_k1, lam_q2, lam_k2, diff_norm_g, conv_w, conv_b, conv_ln_g, conv_ln_b, pool_w, pool_scale, w_branch, w_out, peer_wq, peer_k1, peer_k2, peer_u, peer_v, final_g):
    B, L, D = x.shape
    Lc = ctx.shape[1]
    depth = w_in.shape[0]
    assert L % GRID_W == 0 and L % BLOCK == 0 and Lc % BLOCK == 0 and L % Lc == 0

    n_streams = N_STREAMS if B % N_STREAMS == 0 else 1
    bs = B // n_streams
    n_lat, n_ctx = bs * L, bs * Lc
    n_all = n_lat + n_ctx

    tm_mm = _pick_tile((1024, 512, 256, 128), L, n_ctx)
    tm_tok = _pick_tile((512, 256, 128), L, n_ctx)
    geo = dict(
        B=bs, L=L, Lc=Lc, n_lat=n_lat, n_ctx=n_ctx, n_all=n_all, tm_mm=tm_mm, tm_tok=tm_tok,
        group_map=lambda tm: (lambda i: jnp.where(i < n_lat // tm, (i * tm) // L, bs)),
        pos_map=lambda tm: (lambda i: jnp.where(i < n_lat // tm, i % (L // tm), L // tm)),
    )

    n_mod_rows = -(-(bs + 1) // 8) * 8
    streams = []
    for s in range(n_streams):
        sl = slice(s * bs, (s + 1) * bs)
        streams.append(dict(
            xa=jnp.concatenate([x[sl].reshape(n_lat, D), ctx[sl].reshape(n_ctx, D)], axis=0),
            cvec=jnp.concatenate([c[sl], c_ctx[None, :], jnp.zeros((n_mod_rows - bs - 1, D), F32)], axis=0),
            ffn=None, mod_prev=None))

    head_a = [True] * WA_HEADS + [True] * WA_KV_HEADS + [False] * WA_KV_HEADS
    head_c = [True] * (2 * BRANCH_W // HEAD_DIM) + [False] * (BRANCH_W // HEAD_DIM)
    cos_a, sin_a = _rope_tables(L, tm_mm, head_a)
    cos_c, sin_c = _rope_tables(L, tm_mm, head_c)

    for i in range(depth):
        ctx_out = i < depth - 1
        n_tok = n_all if ctx_out else n_lat
        lam_init = 0.8 - 0.6 * math.exp(-0.3 * i)
        w_in_b = w_in[i].astype(BF16)
        w_a, w_c, w_bd = (w_in_b[:, :A_COLS], w_in_b[:, A_COLS:A_COLS + C_COLS], w_in_b[:, A_COLS + C_COLS:])
        w_g = jnp.transpose(w_gate[i], (1, 0, 2)).reshape(D, N_BRANCH * D).astype(BF16)
        b_g = b_gate[i].reshape(1, N_BRANCH * D)
        w_br, w_o = w_branch[i].astype(BF16), w_out[i].astype(BF16)
        w_q, k1, k2 = peer_wq[i].astype(BF16), peer_k1[i].astype(BF16), peer_k2[i].astype(BF16)
        pool_w_b = pool_w[i].astype(BF16)
        u_packed = _pack_bf16_pairs(lax.bitcast_convert_type(peer_u[i], jnp.int32))
        v_packed = _pack_bf16_pairs(lax.bitcast_convert_type(peer_v[i], jnp.int32))
        lam_params = jnp.stack([lam_q1[i], lam_k1[i], lam_q2[i], lam_k2[i]], axis=0)

        for st in streams:
            mod = mod_vectors(st["cvec"], w_mod[i], b_mod[i]).reshape(n_mod_rows, 6, D)
            if st["ffn"] is None:
                xa = st["xa"]
                (h,) = pre_norm(geo, n_all, xa, None, None, mod, norm1_g[i][None, :])
            else:
                xa, h = pre_norm(geo, n_all, st["xa"], st["ffn"], st["mod_prev"], mod, norm1_g[i][None, :])
            sa = token_matmul(geo, n_all, h, w_a, "rope", (cos_a, sin_a))
            sc = token_matmul(geo, n_all, h, w_c, "rope", (cos_c, sin_c))
            ya = window_attention(geo, sa, attn_sink[i], ctx_out)
            yc = diff_attention(geo, sc, lam_params, diff_norm_g[i][None, :], lam_init, ctx_out)
            bd = token_matmul(geo, n_tok, h, w_bd)
            yb, yd = local_mixers(geo, n_tok, bd, conv_w[i], conv_b[i][None, :], conv_ln_g[i][None, :],
                                  conv_ln_b[i][None, :], pool_w_b, pool_scale[i][None, :])
            gates = token_matmul(geo, n_tok, h, w_g, "sigmoid", (b_g,))
            xa, h2b, h2p = merge_branches(geo, n_tok, ya, yc, yb, yd, gates, xa, mod, norm2_g[i][None, :],
                                          w_br, w_o)
            idx, gate = peer_route(geo, n_tok, h2b, w_q, k1, k2)
            dots = sc_peer_dots(h2p, idx, u_packed, n_tok)
            wgt = peer_activation(geo, n_tok, gate, dots)
            st.update(xa=xa, ffn=sc_peer_wsum(wgt, idx, v_packed, n_tok), mod_prev=mod)

    outs = [pre_norm(geo, n_lat, st["xa"], st["ffn"], st["mod_prev"], st["mod_prev"], final_g[None, :],
                     final=True)[0].reshape(bs, L, D) for st in streams]
    return jnp.concatenate(outs, axis=0)
```

```python
import functools
import math

import jax
import jax.numpy as jnp
from jax import lax
from jax.experimental import pallas as pl
from jax.experimental.pallas import tpu as pltpu
from jax.experimental.pallas import tpu_sc as plsc

GRID_W = 64
HEAD_DIM = 64
ROPE_THETA = 10000.0
BLOCK = 128
WINDOW = 128
N_BRANCH = 4
WA_HEADS = 8
WA_KV_HEADS = 2
WA_GROUP = WA_HEADS // WA_KV_HEADS
DIFF_HEADS = 4
CONV_K = 31
POOL_WINDOWS = (2, 4, 8, 16)
PEER_HEADS = 8
PEER_NKEYS = 128
PEER_TOPK = 16
EPS = 1e-6
NEG_INF = -1e30
NEG_BIG = -3.0e38
BRANCH_W = WA_HEADS * HEAD_DIM
A_COLS = BRANCH_W + 2 * WA_KV_HEADS * HEAD_DIM
C_COLS = 3 * BRANCH_W
BD_COLS = 3 * BRANCH_W

LANES = 128
VMEM_LIMIT = 48 * 1024 * 1024
SC_LANES = 16
SC_WORKERS = 32
PEER_ROWS = PEER_HEADS * PEER_TOPK
GATHER_ROWS = 32
GATHER_BUFFERS = 4
BF16_SUM = 4
SC_TOKEN_BLOCK = 16
N_STREAMS = 4

F32 = jnp.float32
BF16 = jnp.bfloat16


def _cparams(*sem):
    return pltpu.CompilerParams(dimension_semantics=sem, vmem_limit_bytes=VMEM_LIMIT)


def _pick_tile(candidates, *sizes):
    for t in candidates:
        if all(s % t == 0 for s in sizes):
            return t
    raise ValueError(f"no tile in {candidates} divides {sizes}")


def _rms(x):
    return x * lax.rsqrt(jnp.mean(x * x, axis=-1, keepdims=True) + EPS)


def _pack_bf16_pairs(bits):
    rounded = bits + 0x7FFF + (lax.shift_right_logical(bits, 16) & 1)
    half = bits.shape[1] // 2
    return (rounded[:, half:] & jnp.int32(-65536)) | lax.shift_right_logical(rounded[:, :half], 16)


def _sc_mesh():
    return plsc.VectorSubcoreMesh(core_axis_name="c", subcore_axis_name="s")


def _sc_worker_id():
    return lax.axis_index("s") * 2 + lax.axis_index("c")


def _lane_broadcast(vec, lane, r):
    return jnp.full((SC_LANES,), jnp.sum(jnp.where(lane == r, vec, 0.0)), F32)


def _tree_sum(parts):
    while len(parts) > 1:
        parts = [parts[i] + parts[i + 1] for i in range(0, len(parts), 2)]
    return parts[0]


def _sc_peer_call(kind, a, idx, table, n_tokens):
    T = n_tokens
    DP = table.shape[1]
    D = 2 * DP
    TB, GR, NBUF = SC_TOKEN_BLOCK, GATHER_ROWS, GATHER_BUFFERS
    assert T % (SC_WORKERS * TB) == 0 and DP % (SC_LANES * BF16_SUM) == 0 and idx.shape[1] == PEER_ROWS
    tpw = T // SC_WORKERS
    n_chunks = DP // SC_LANES
    n_groups = PEER_ROWS // GR
    n_steps = TB * n_groups
    a_width = a.shape[1]
    out_width = PEER_ROWS if kind == "dots" else D

    def body(a_hbm, idx_hbm, tab_hbm, out_hbm, idx_v, a_v, rows_v, out_v, sems):
        wid = _sc_worker_id()
        lane = lax.iota(jnp.int32, SC_LANES)

        def gather(step):
            t, g, slot = step // n_groups, step % n_groups, step % NBUF
            return pltpu.make_async_copy(tab_hbm.at[idx_v.at[t, pl.ds(g * GR, GR)]],
                                         rows_v.at[pl.ds(slot * GR, GR)], sems.at[slot])

        def packed(ref, r, c):
            return plsc.bitcast(ref[r, pl.ds(c * SC_LANES, SC_LANES)], BF16)

        def unpack_f32(p):
            return plsc.unpack(p, format=plsc.PackFormat.INTERLEAVED, preferred_element_type=F32)

        def dots_group(t, g, base):
            xs = [packed(a_v, t, k) for k in range(n_chunks)]
            for half in range(GR // SC_LANES):
                r0 = base + half * SC_LANES

                def row(r, tot):
                    parts = []
                    for k0 in range(0, n_chunks, BF16_SUM):
                        prods = [xs[k0 + k] * packed(rows_v, r0 + r, k0 + k) for k in range(BF16_SUM)]
                        lo, hi = unpack_f32(_tree_sum(prods))
                        parts.append(lo + hi)
                    return jnp.where(lane == r, jnp.sum(_tree_sum(parts)), tot)

                tot = plsc.parallel_loop(0, SC_LANES, 1, carry=jnp.zeros((SC_LANES,), F32))(row)
                out_v[t, pl.ds(g * GR + half * SC_LANES, SC_LANES)] = tot

        def wsum_group(t, g, base):
            for half in range(GR // SC_LANES):
                r0 = base + half * SC_LANES
                wvec = a_v[t, pl.ds(g * GR + half * SC_LANES, SC_LANES)]
                wb = []
                for r in range(SC_LANES):
                    s = _lane_broadcast(wvec, lane, r)
                    wb.append(plsc.pack(s, s, format=plsc.PackFormat.INTERLEAVED, preferred_element_type=BF16))
                first = (g == 0 and half == 0)

                @plsc.parallel_loop(0, n_chunks, 1, unroll=2)
                def _(c):
                    lo_parts, hi_parts = [], []
                    for r in range(0, SC_LANES, 2):
                        lo, hi = unpack_f32(wb[r] * packed(rows_v, r0 + r, c)
                                            + wb[r + 1] * packed(rows_v, r0 + r + 1, c))
                        lo_parts.append(lo)
                        hi_parts.append(hi)
                    for sl, s in ((pl.ds(c * SC_LANES, SC_LANES), _tree_sum(lo_parts)),
                                  (pl.ds(D // 2 + c * SC_LANES, SC_LANES), _tree_sum(hi_parts))):
                        out_v[t, sl] = s if first else out_v[t, sl] + s

        group = dots_group if kind == "dots" else wsum_group

        @pl.loop(0, tpw // TB)
        def _(b):
            tok0 = wid * tpw + b * TB
            pltpu.sync_copy(idx_hbm.at[pl.ds(tok0, TB)], idx_v)
            pltpu.sync_copy(a_hbm.at[pl.ds(tok0, TB)], a_v)
            for p in range(NBUF - 1):
                gather(p).start()

            @pl.loop(0, TB)
            def _(t):
                for g in range(n_groups):
                    step = t * n_groups + g

                    @pl.when(step + NBUF - 1 < n_steps)
                    def _():
                        gather(step + NBUF - 1).start()

                    gather(step).wait()
                    group(t, g, (step % NBUF) * GR)

            pltpu.sync_copy(out_v, out_hbm.at[pl.ds(tok0, TB)])

    f = pl.kernel(
        body,
        out_type=jax.ShapeDtypeStruct((T, out_width), F32),
        mesh=_sc_mesh(),
        compiler_params=pltpu.CompilerParams(needs_layout_passes=False),
        scratch_types=[
            pltpu.VMEM((TB, PEER_ROWS), jnp.int32),
            pltpu.VMEM((TB, a_width), a.dtype),
            pltpu.VMEM((NBUF * GR, DP), jnp.int32),
            pltpu.VMEM((TB, out_width), F32),
            pltpu.SemaphoreType.DMA((NBUF,)),
        ],
    )
    return f(a, idx, table)


def sc_peer_dots(x, idx, table, n_tokens):
    return _sc_peer_call("dots", x, idx, table, n_tokens)


def sc_peer_wsum(w, idx, table, n_tokens):
    return _sc_peer_call("wsum", w, idx, table, n_tokens)


def _mod_kernel(c_ref, w_ref, b_ref, o_ref):
    cv = c_ref[...]
    s = cv * jax.nn.sigmoid(cv)
    o_ref[...] = jnp.dot(s, w_ref[...], precision=lax.Precision.HIGHEST,
                         preferred_element_type=F32) + b_ref[...]


def mod_vectors(cvec, w_mod, b_mod):
    R, D = cvec.shape
    W = w_mod.shape[1]
    tn = _pick_tile((512, 256, 128), W)
    return pl.pallas_call(
        _mod_kernel,
        grid=(W // tn,),
        in_specs=[pl.BlockSpec((R, D), lambda j: (0, 0)),
                  pl.BlockSpec((D, tn), lambda j: (0, j)),
                  pl.BlockSpec((1, tn), lambda j: (0, j))],
        out_specs=pl.BlockSpec((R, tn), lambda j: (0, j)),
        out_shape=jax.ShapeDtypeStruct((R, W), F32),
        compiler_params=_cparams("arbitrary"),
        name="mod_vectors",
    )(cvec, w_mod, b_mod.reshape(1, W))


def _pre_kernel(has_ffn, final, *refs):
    if has_ffn:
        x_ref, ffn_ref, modp_ref, mod_ref, g_ref = refs[:5]
        outs = refs[5:]
        x = x_ref[...] + modp_ref[0, 5:6, :] * ffn_ref[...]
    else:
        x_ref, mod_ref, g_ref = refs[:3]
        outs = refs[3:]
        x = x_ref[...]
    y = _rms(x) * g_ref[...]
    if final:
        outs[0][...] = y
        return
    h = y * (1.0 + mod_ref[0, 1:2, :]) + mod_ref[0, 0:1, :]
    if has_ffn:
        outs[0][...] = x
        outs[1][...] = h.astype(BF16)
    else:
        outs[0][...] = h.astype(BF16)


def pre_norm(geo, n_tokens, x, ffn, mod_prev, mod, gain, final=False):
    D = x.shape[1]
    tm = geo["tm_tok"]
    grp = geo["group_map"](tm)
    has_ffn = ffn is not None
    row = pl.BlockSpec((tm, D), lambda i: (i, 0))
    modspec = pl.BlockSpec((1, 6, D), lambda i: (grp(i), 0, 0))
    gspec = pl.BlockSpec((1, D), lambda i: (0, 0))
    if has_ffn:
        args = (x, ffn, mod_prev, mod, gain)
        in_specs = [row, row, modspec, modspec, gspec]
    else:
        args = (x, mod, gain)
        in_specs = [row, modspec, gspec]
    n_all = x.shape[0]
    if final:
        out_shape = [jax.ShapeDtypeStruct((n_tokens, D), F32)]
    elif has_ffn:
        out_shape = [jax.ShapeDtypeStruct((n_all, D), F32), jax.ShapeDtypeStruct((n_all, D), BF16)]
    else:
        out_shape = [jax.ShapeDtypeStruct((n_all, D), BF16)]
    outs = pl.pallas_call(
        functools.partial(_pre_kernel, has_ffn, final),
        grid=(n_tokens // tm,),
        in_specs=in_specs,
        out_specs=[row] * len(out_shape),
        out_shape=out_shape,
        compiler_params=_cparams("parallel"),
        name="pre_norm",
    )(*args)
    return outs


def _mm_kernel(epi, tn, a_ref, w_ref, *rest):
    acc = jnp.dot(a_ref[...], w_ref[...], preferred_element_type=F32)
    if epi == "rope":
        cos_ref, sin_ref, o_ref = rest
        for ch in range(tn // LANES):
            sl = slice(ch * LANES, (ch + 1) * LANES)
            a = acc[:, sl]
            lane = lax.broadcasted_iota(jnp.int32, a.shape, 1)
            partner = jnp.where((lane % HEAD_DIM) < HEAD_DIM // 2,
                                pltpu.roll(a, LANES - HEAD_DIM // 2, 1), pltpu.roll(a, HEAD_DIM // 2, 1))
            o_ref[:, sl] = (a * cos_ref[:, sl] + partner * sin_ref[:, sl]).astype(o_ref.dtype)
    elif epi == "sigmoid":
        b_ref, o_ref = rest
        o_ref[...] = jax.nn.sigmoid(acc + b_ref[...]).astype(o_ref.dtype)
    else:
        (o_ref,) = rest
        o_ref[...] = acc.astype(o_ref.dtype)


def token_matmul(geo, n_tokens, a, w, epi="none", extras=()):
    n_all, K = a.shape
    W = w.shape[1]
    tm = geo["tm_mm"]
    tn = _pick_tile((768, 512, 256, 128), W)
    in_specs = [pl.BlockSpec((tm, K), lambda i, j: (i, 0)), pl.BlockSpec((K, tn), lambda i, j: (0, j))]
    if epi == "rope":
        pos = geo["pos_map"](tm)
        in_specs += [pl.BlockSpec((tm, tn), lambda i, j: (pos(i), j))] * 2
    elif epi == "sigmoid":
        in_specs += [pl.BlockSpec((1, tn), lambda i, j: (0, j))]
    return pl.pallas_call(
        functools.partial(_mm_kernel, epi, tn),
        grid=(n_tokens // tm, W // tn),
        in_specs=in_specs,
        out_specs=pl.BlockSpec((tm, tn), lambda i, j: (i, j)),
        out_shape=jax.ShapeDtypeStruct((n_all, W), BF16),
        compiler_params=_cparams("parallel", "arbitrary"),
        name="token_matmul_" + epi,
    )(a, w, *extras)


def _attn_a_kernel(nb, seq_len, sink_ref, cur_ref, prev_ref, next_ref, ctx_ref, o_ref):
    n = pl.program_id(1)
    is_lat = n < nb
    qi = lax.broadcasted_iota(jnp.int32, (BLOCK, 3 * BLOCK), 0)
    kj = lax.broadcasted_iota(jnp.int32, (BLOCK, 3 * BLOCK), 1)
    kabs = (n - 1) * BLOCK + kj
    band = (jnp.abs(kj - BLOCK - qi) <= WINDOW) & (kabs >= 0) & (kabs < seq_len) & is_lat
    nt = (((1,), (1,)), ((), ()))
    kv_off = WA_HEADS * HEAD_DIM
    for kvh in range(WA_KV_HEADS):
        kc = slice(kv_off + kvh * HEAD_DIM, kv_off + (kvh + 1) * HEAD_DIM)
        vc = slice(kv_off + (WA_KV_HEADS + kvh) * HEAD_DIM, kv_off + (WA_KV_HEADS + kvh + 1) * HEAD_DIM)
        k_lat = jnp.concatenate([prev_ref[:, kc], cur_ref[:, kc], next_ref[:, kc]], axis=0)
        v_lat = jnp.concatenate([prev_ref[:, vc], cur_ref[:, vc], next_ref[:, vc]], axis=0)
        k_ctx = ctx_ref[:, kc]
        v_ctx = ctx_ref[:, vc]
        for g in range(WA_GROUP):
            h = kvh * WA_GROUP + g
            qh = cur_ref[:, h * HEAD_DIM:(h + 1) * HEAD_DIM]
            s1 = lax.dot_general(qh, k_lat, nt, preferred_element_type=F32) * (HEAD_DIM ** -0.5)
            s1 = jnp.where(band, s1, NEG_INF)
            s2 = lax.dot_general(qh, k_ctx, nt, preferred_element_type=F32) * (HEAD_DIM ** -0.5)
            sink = sink_ref[h]
            m = jnp.maximum(jnp.maximum(jnp.max(s1, axis=1, keepdims=True),
                                        jnp.max(s2, axis=1, keepdims=True)), sink)
            e1 = jnp.exp(s1 - m)
            e2 = jnp.exp(s2 - m)
            den = jnp.sum(e1, axis=1, keepdims=True) + jnp.sum(e2, axis=1, keepdims=True) + jnp.exp(sink - m)
            pv = (jnp.dot(e1.astype(BF16), v_lat, preferred_element_type=F32)
                  + jnp.dot(e2.astype(BF16), v_ctx, preferred_element_type=F32))
            o_ref[:, h * HEAD_DIM:(h + 1) * HEAD_DIM] = (pv / den).astype(o_ref.dtype)


def window_attention(geo, sa, sink, ctx_out):
    B, L, Lc, n_lat, n_all = geo["B"], geo["L"], geo["Lc"], geo["n_lat"], geo["n_all"]
    nb, ncb = L // BLOCK, Lc // BLOCK
    lat0, ctx0 = n_lat // BLOCK, n_lat // Lc

    def cur(b, n):
        return jnp.where(n < nb, b * nb + n, lat0 + b * ncb + (n - nb))

    def prev(b, n):
        return jnp.where(n < nb, b * nb + jnp.maximum(n - 1, 0), lat0 + b * ncb + (n - nb))

    def nxt(b, n):
        return jnp.where(n < nb, b * nb + jnp.minimum(n + 1, nb - 1), lat0 + b * ncb + (n - nb))

    blk = (BLOCK, A_COLS)
    return pl.pallas_call(
        functools.partial(_attn_a_kernel, nb, L),
        grid=(B, nb + (ncb if ctx_out else 0)),
        in_specs=[pl.BlockSpec(memory_space=pltpu.SMEM),
                  pl.BlockSpec(blk, lambda b, n: (cur(b, n), 0)),
                  pl.BlockSpec(blk, lambda b, n: (prev(b, n), 0)),
                  pl.BlockSpec(blk, lambda b, n: (nxt(b, n), 0)),
                  pl.BlockSpec((Lc, A_COLS), lambda b, n: (ctx0 + b, 0))],
        out_specs=pl.BlockSpec((BLOCK, BRANCH_W), lambda b, n: (cur(b, n), 0)),
        out_shape=jax.ShapeDtypeStruct((n_all, BRANCH_W), BF16),
        compiler_params=_cparams("parallel", "arbitrary"),
        name="window_attention",
    )(sink, sa, sa, sa, sa)


def _attn_c_kernel(seq_len, ctx_len, tq, lam_init, lat_ref, ctx_ref, lam_ref, g_ref, o_ref, kall, vall):
    n = pl.program_id(1)
    nlq = seq_len // tq
    kcol = slice(BRANCH_W, 2 * BRANCH_W)
    vcol = slice(2 * BRANCH_W, 3 * BRANCH_W)

    @pl.when(n == 0)
    def _():
        kall[0:seq_len, :] = lat_ref[:, kcol]
        kall[seq_len:seq_len + ctx_len, :] = ctx_ref[:, kcol]
        vall[0:seq_len, :] = lat_ref[:, vcol]
        vall[seq_len:seq_len + ctx_len, :] = ctx_ref[:, vcol]

    lam = (jnp.exp(jnp.sum(lam_ref[0:1, :] * lam_ref[1:2, :], axis=1, keepdims=True))
           - jnp.exp(jnp.sum(lam_ref[2:3, :] * lam_ref[3:4, :], axis=1, keepdims=True)) + lam_init)
    nt = (((1,), (1,)), ((), ()))

    def heads(q, k0, klen):
        for h in range(DIFF_HEADS):
            probs = []
            for c in range(2):
                col = slice(h * 2 * HEAD_DIM + c * HEAD_DIM, h * 2 * HEAD_DIM + (c + 1) * HEAD_DIM)
                s = lax.dot_general(q[:, col], kall[k0:k0 + klen, col], nt,
                                    preferred_element_type=F32) * (HEAD_DIM ** -0.5)
                e = jnp.exp(s - jnp.max(s, axis=1, keepdims=True))
                probs.append(e * (1.0 / jnp.sum(e, axis=1, keepdims=True)))
            a = probs[0] - lam * probs[1]
            hv = slice(h * 2 * HEAD_DIM, (h + 1) * 2 * HEAD_DIM)
            o = jnp.dot(a.astype(BF16), vall[k0:k0 + klen, hv], preferred_element_type=F32)
            o_ref[:, hv] = (_rms(o) * g_ref[...] * (1.0 - lam_init)).astype(o_ref.dtype)

    @pl.when(n < nlq)
    def _():
        heads(lat_ref[pl.ds(pl.multiple_of(n * tq, tq), tq), 0:BRANCH_W], 0, seq_len + ctx_len)

    @pl.when(n >= nlq)
    def _():
        heads(ctx_ref[pl.ds(pl.multiple_of((n - nlq) * tq, tq), tq), 0:BRANCH_W], seq_len, ctx_len)


def diff_attention(geo, sc, lam_params, diff_g, lam_init, ctx_out):
    B, L, Lc, n_lat, n_all = geo["B"], geo["L"], geo["Lc"], geo["n_lat"], geo["n_all"]
    tq = _pick_tile((256, 128), L, Lc)
    nlq, ncq = L // tq, Lc // tq
    lat0, ctx0 = n_lat // tq, n_lat // Lc

    def orow(b, n):
        return jnp.where(n < nlq, b * nlq + n, lat0 + b * ncq + (n - nlq))

    return pl.pallas_call(
        functools.partial(_attn_c_kernel, L, Lc, tq, lam_init),
        grid=(B, nlq + (ncq if ctx_out else 0)),
        in_specs=[pl.BlockSpec((L, C_COLS), lambda b, n: (b, 0)),
                  pl.BlockSpec((Lc, C_COLS), lambda b, n: (ctx0 + b, 0)),
                  pl.BlockSpec((4, HEAD_DIM), lambda b, n: (0, 0)),
                  pl.BlockSpec((1, 2 * HEAD_DIM), lambda b, n: (0, 0))],
        out_specs=pl.BlockSpec((tq, BRANCH_W), lambda b, n: (orow(b, n), 0)),
        out_shape=jax.ShapeDtypeStruct((n_all, BRANCH_W), BF16),
        scratch_shapes=[pltpu.VMEM((L + Lc, BRANCH_W), BF16), pltpu.VMEM((L + Lc, BRANCH_W), BF16)],
        compiler_params=_cparams("parallel", "arbitrary"),
        name="diff_attention",
    )(sc, sc, lam_params, diff_g)


HALO = 16
CONV_ROWS = 64


def _local_kernel(tile, seq_len, ctx_len, n_lat_tiles, cur_ref, prev_ref, next_ref, cw_ref, cb_ref,
                  lg_ref, lb_ref, pw_ref, ps_ref, yb_ref, yd_ref, ypad, xpad):
    i = pl.program_id(0)
    tiles_per_seq = seq_len // tile
    is_lat = i < n_lat_tiles
    pos = jnp.where(is_lat, i % tiles_per_seq, 0)
    prev_ok = is_lat & (pos > 0)
    next_ok = is_lat & (pos < tiles_per_seq - 1)
    cc = BRANCH_W

    def glu(ref):
        return ref[:, 0:cc].astype(F32) * jax.nn.sigmoid(ref[:, cc:2 * cc].astype(F32))

    def pool_in(ref):
        return ref[:, 2 * cc:3 * cc].astype(F32)

    ypad[0:HALO, :] = jnp.where(prev_ok, glu(prev_ref), 0.0)
    ypad[HALO:HALO + tile, :] = glu(cur_ref)
    ypad[HALO + tile:2 * HALO + tile, :] = jnp.where(next_ok, glu(next_ref), 0.0)
    xpad[0:HALO, :] = jnp.where(prev_ok, pool_in(prev_ref), 0.0)
    xpad[HALO:HALO + tile, :] = pool_in(cur_ref)
    xpad[HALO + tile:2 * HALO + tile, :] = jnp.where(next_ok, pool_in(next_ref), 0.0)

    for r0 in range(0, tile, CONV_ROWS):
        acc = jnp.zeros((CONV_ROWS, cc), F32)
        for k in range(CONV_K):
            start = r0 + HALO - CONV_K // 2 + k
            acc = acc + cw_ref[k:k + 1, :] * ypad[start:start + CONV_ROWS, :]
        y = acc + cb_ref[...]
        mu = jnp.mean(y, axis=-1, keepdims=True)
        yc = y - mu
        var = jnp.mean(yc * yc, axis=-1, keepdims=True)
        z = yc * lax.rsqrt(var + EPS) * lg_ref[...] + lb_ref[...]
        yb_ref[r0:r0 + CONV_ROWS, :] = (z * jax.nn.sigmoid(z)).astype(yb_ref.dtype)

    gc = cc // len(POOL_WINDOWS)
    tpos = pos * tile + lax.broadcasted_iota(jnp.int32, (tile, 1), 0)
    last = jnp.where(is_lat, seq_len, ctx_len) - 1
    for g, win in enumerate(POOL_WINDOWS):
        lo, hi = win // 2, win - 1 - win // 2
        cols = slice(g * gc, (g + 1) * gc)
        wsum = jnp.zeros((tile, gc), F32)
        for j in range(-lo, hi + 1):
            wsum = wsum + xpad[HALO + j:HALO + j + tile, cols]
        cnt = (jnp.minimum(tpos + hi, last) - jnp.maximum(tpos - lo, 0) + 1).astype(F32)
        y = wsum / cnt - xpad[HALO:HALO + tile, cols]
        proj = jnp.dot(y.astype(BF16), pw_ref[g], preferred_element_type=F32)
        yd_ref[:, cols] = (proj * ps_ref[:, cols]).astype(yd_ref.dtype)


def local_mixers(geo, n_tokens, bd, conv_w, conv_b, ln_g, ln_b, pool_w, pool_scale):
    L, Lc, n_lat, n_all = geo["L"], geo["Lc"], geo["n_lat"], geo["n_all"]
    tile = Lc
    assert L % tile == 0 and tile % CONV_ROWS == 0 and tile % HALO == 0
    hb = tile // HALO
    n_halo_blocks = n_all // HALO
    cc = BRANCH_W
    vec = pl.BlockSpec((1, cc), lambda i: (0, 0))
    out = pl.BlockSpec((tile, cc), lambda i: (i, 0))
    return pl.pallas_call(
        functools.partial(_local_kernel, tile, L, Lc, n_lat // tile),
        grid=(n_tokens // tile,),
        in_specs=[pl.BlockSpec((tile, BD_COLS), lambda i: (i, 0)),
                  pl.BlockSpec((HALO, BD_COLS), lambda i: (jnp.maximum(i * hb - 1, 0), 0)),
                  pl.BlockSpec((HALO, BD_COLS), lambda i: (jnp.minimum((i + 1) * hb, n_halo_blocks - 1), 0)),
                  pl.BlockSpec((CONV_K, cc), lambda i: (0, 0)),
                  vec, vec, vec,
                  pl.BlockSpec((len(POOL_WINDOWS), cc // len(POOL_WINDOWS), cc // len(POOL_WINDOWS)),
                               lambda i: (0, 0, 0)),
                  vec],
        out_specs=[out, out],
        out_shape=[jax.ShapeDtypeStruct((n_all, cc), BF16)] * 2,
        scratch_shapes=[pltpu.VMEM((tile + 2 * HALO, cc), F32), pltpu.VMEM((tile + 2 * HALO, cc), F32)],
        compiler_params=_cparams("parallel"),
        name="local_mixers",
    )(bd, bd, bd, conv_w, conv_b, ln_g, ln_b, pool_w, pool_scale)


def _merge_kernel(ya_ref, yc_ref, yb_ref, yd_ref, gate_ref, x_ref, mod_ref, g_ref, wb_ref, wo_ref,
                  x1_ref, h2b_ref, h2p_ref):
    d = x_ref.shape[1]
    acc = None
    for n, y_ref in enumerate((ya_ref, yc_ref, yb_ref, yd_ref)):
        term = gate_ref[:, n * d:(n + 1) * d].astype(F32) * jnp.dot(y_ref[...], wb_ref[n],
                                                                   preferred_element_type=F32)
        acc = term if acc is None else acc + term
    out = jnp.dot(acc.astype(BF16), wo_ref[...], preferred_element_type=F32)
    x1 = x_ref[...] + mod_ref[0, 2:3, :] * out
    x1_ref[...] = x1
    h2 = _rms(x1) * g_ref[...] * (1.0 + mod_ref[0, 4:5, :]) + mod_ref[0, 3:4, :]
    h2b = h2.astype(BF16)
    h2b_ref[...] = h2b
    h2p_ref[...] = _pack_bf16_pairs(pltpu.bitcast(h2, jnp.int32))


def merge_branches(geo, n_tokens, ya, yc, yb, yd, gates, x, mod, gain2, w_branch, w_out):
    n_all, D = x.shape
    tm = geo["tm_tok"]
    grp = geo["group_map"](tm)
    cc = BRANCH_W
    yspec = pl.BlockSpec((tm, cc), lambda i: (i, 0))
    row = pl.BlockSpec((tm, D), lambda i: (i, 0))
    return pl.pallas_call(
        _merge_kernel,
        grid=(n_tokens // tm,),
        in_specs=[yspec, yspec, yspec, yspec,
                  pl.BlockSpec((tm, N_BRANCH * D), lambda i: (i, 0)),
                  row,
                  pl.BlockSpec((1, 6, D), lambda i: (grp(i), 0, 0)),
                  pl.BlockSpec((1, D), lambda i: (0, 0)),
                  pl.BlockSpec((N_BRANCH, cc, D), lambda i: (0, 0, 0)),
                  pl.BlockSpec((D, D), lambda i: (0, 0))],
        out_specs=[row, row, pl.BlockSpec((tm, D // 2), lambda i: (i, 0))],
        out_shape=[jax.ShapeDtypeStruct((n_all, D), F32), jax.ShapeDtypeStruct((n_all, D), BF16),
                   jax.ShapeDtypeStruct((n_all, D // 2), jnp.int32)],
        compiler_params=_cparams("parallel"),
        name="merge_branches",
    )(ya, yc, yb, yd, gates, x, mod, gain2, w_branch, w_out)


_CAND_PAIRS = [(a, b) for a in range(PEER_TOPK) for b in range(PEER_TOPK) if (a + 1) * (b + 1) <= PEER_TOPK]
_CAND_ROWS = -(-len(_CAND_PAIRS) // 8) * 8


def _extract_top(s, row, n_rows):
    m = jnp.max(s, axis=0, keepdims=True)
    am = jnp.min(jnp.where(s == m, row, n_rows), axis=0, keepdims=True)
    return m, am


def _route_kernel(h_ref, wq_ref, k1_ref, k2_ref, idx_ref, gate_ref,
                  v1_s, i1_s, v2_s, i2_s, cand_s, cidx_s, top_s, idxt_s, gatet_s):
    tm = h_ref.shape[0]
    q = jnp.dot(h_ref[...], wq_ref[...], preferred_element_type=F32).astype(BF16)
    nt = (((1,), (1,)), ((), ()))
    key_row = lax.broadcasted_iota(jnp.int32, (PEER_NKEYS, tm), 0)
    cand_row = lax.broadcasted_iota(jnp.int32, (_CAND_ROWS, tm), 0)
    if _CAND_ROWS > len(_CAND_PAIRS):
        cand_s[len(_CAND_PAIRS):_CAND_ROWS, :] = jnp.full((_CAND_ROWS - len(_CAND_PAIRS), tm), NEG_BIG, F32)
        cidx_s[len(_CAND_PAIRS):_CAND_ROWS, :] = jnp.zeros((_CAND_ROWS - len(_CAND_PAIRS), tm), jnp.int32)
    for h in range(PEER_HEADS):
        for c, (k_ref, v_s, i_s) in enumerate(((k1_ref, v1_s, i1_s), (k2_ref, v2_s, i2_s))):
            col = (h * 2 + c) * PEER_NKEYS
            s = lax.dot_general(k_ref[h], q[:, col:col + PEER_NKEYS], nt, preferred_element_type=F32)
            for r in range(PEER_TOPK):
                m, am = _extract_top(s, key_row, PEER_NKEYS)
                v_s[r:r + 1, :] = m
                i_s[r:r + 1, :] = am
                s = jnp.where(key_row == am, NEG_BIG, s)
        a_prev, off = -1, 0
        for a in range(PEER_TOPK):
            nb_a = PEER_TOPK // (a + 1)
            cand_s[off:off + nb_a, :] = v1_s[a:a + 1, :] + v2_s[0:nb_a, :]
            cidx_s[off:off + nb_a, :] = i1_s[a:a + 1, :] * PEER_NKEYS + i2_s[0:nb_a, :]
            off += nb_a
        cand = cand_s[...]
        cidx = cidx_s[...]
        for r in range(PEER_TOPK):
            m, am = _extract_top(cand, cand_row, _CAND_ROWS)
            hit = cand_row == am
            top_s[r:r + 1, :] = m
            idxt_s[h * PEER_TOPK + r:h * PEER_TOPK + r + 1, :] = jnp.sum(jnp.where(hit, cidx, 0), axis=0,
                                                                          keepdims=True)
            cand = jnp.where(hit, NEG_BIG, cand)
        top = top_s[...]
        e = jnp.exp(top - jnp.max(top, axis=0, keepdims=True))
        gatet_s[h * PEER_TOPK:(h + 1) * PEER_TOPK, :] = e / jnp.sum(e, axis=0, keepdims=True)
    idx_ref[...] = idxt_s[...].T
    gate_ref[...] = gatet_s[...].T


def peer_route(geo, n_tokens, h2, wq, k1, k2):
    n_all, D = h2.shape
    tm = _pick_tile((256, 128), geo["n_lat"], geo["n_ctx"])
    qw = wq.shape[1]
    kspec = pl.BlockSpec((PEER_HEADS, PEER_NKEYS, qw // (2 * PEER_HEADS)), lambda i: (0, 0, 0))
    out = pl.BlockSpec((tm, PEER_ROWS), lambda i: (i, 0))
    return pl.pallas_call(
        _route_kernel,
        grid=(n_tokens // tm,),
        in_specs=[pl.BlockSpec((tm, D), lambda i: (i, 0)),
                  pl.BlockSpec((D, qw), lambda i: (0, 0)),
                  kspec, kspec],
        out_specs=[out, out],
        out_shape=[jax.ShapeDtypeStruct((n_all, PEER_ROWS), jnp.int32),
                   jax.ShapeDtypeStruct((n_all, PEER_ROWS), F32)],
        scratch_shapes=[pltpu.VMEM((PEER_TOPK, tm), F32), pltpu.VMEM((PEER_TOPK, tm), jnp.int32),
                        pltpu.VMEM((PEER_TOPK, tm), F32), pltpu.VMEM((PEER_TOPK, tm), jnp.int32),
                        pltpu.VMEM((_CAND_ROWS, tm), F32), pltpu.VMEM((_CAND_ROWS, tm), jnp.int32),
                        pltpu.VMEM((PEER_TOPK, tm), F32),
                        pltpu.VMEM((PEER_ROWS, tm), jnp.int32), pltpu.VMEM((PEER_ROWS, tm), F32)],
        compiler_params=_cparams("parallel"),
        name="peer_route",
    )(h2, wq, k1, k2)


def _act_kernel(gate_ref, dots_ref, o_ref):
    d = dots_ref[...]
    o_ref[...] = gate_ref[...] * (0.5 * d * (1.0 + lax.erf(d * (2.0 ** -0.5))))


def peer_activation(geo, n_tokens, gate, dots):
    tm = geo["tm_mm"]
    spec = pl.BlockSpec((tm, PEER_ROWS), lambda i: (i, 0))
    return pl.pallas_call(
        _act_kernel,
        grid=(n_tokens // tm,),
        in_specs=[spec, spec],
        out_specs=spec,
        out_shape=jax.ShapeDtypeStruct((n_tokens, PEER_ROWS), F32),
        compiler_params=_cparams("parallel"),
        name="peer_activation",
    )(gate, dots)


def _rope_tables(seq_len, pad_rows, rope_cols):
    rows = seq_len // GRID_W
    row = jnp.repeat(jnp.arange(rows, dtype=F32), GRID_W)
    col = jnp.tile(jnp.arange(GRID_W, dtype=F32), rows)
    n_freq = HEAD_DIM // 4
    inv_freq = ROPE_THETA ** (-jnp.arange(n_freq, dtype=F32) / n_freq)
    ang = jnp.concatenate([row[:, None] * inv_freq, col[:, None] * inv_freq], axis=-1)
    cos_h = jnp.concatenate([jnp.cos(ang), jnp.cos(ang)], axis=-1)
    sin_h = jnp.concatenate([-jnp.sin(ang), jnp.sin(ang)], axis=-1)
    one, zero = jnp.ones_like(cos_h), jnp.zeros_like(sin_h)
    cos = jnp.concatenate([cos_h if r else one for r in rope_cols], axis=-1)
    sin = jnp.concatenate([sin_h if r else zero for r in rope_cols], axis=-1)
    w = cos.shape[1]
    cos = jnp.concatenate([cos, jnp.ones((pad_rows, w), F32)], axis=0)
    sin = jnp.concatenate([sin, jnp.zeros((pad_rows, w), F32)], axis=0)
    return cos, sin


def kernel(x, c, ctx, c_ctx, w_mod, b_mod, norm1_g, norm2_g, w_in, w_gate, b_gate, attn_sink, lam_q1, lam_k1, lam_q2, lam_k2, diff_norm_g, conv_w, conv_b, conv_ln_g, conv_ln_b, pool_w, pool_scale, w_branch, w_out, peer_wq, peer_k1, peer_k2, peer_u, peer_v, final_g):
    B, L, D = x.shape
    Lc = ctx.shape[1]
    depth = w_in.shape[0]
    assert L % GRID_W == 0 and L % BLOCK == 0 and Lc % BLOCK == 0 and L % Lc == 0

    n_streams = N_STREAMS if B % N_STREAMS == 0 else 1
    bs = B // n_streams
    n_lat, n_ctx = bs * L, bs * Lc
    n_all = n_lat + n_ctx

    tm_mm = _pick_tile((1024, 512, 256, 128), L, n_ctx)
    tm_tok = _pick_tile((512, 256, 128), L, n_ctx)
    geo = dict(
        B=bs, L=L, Lc=Lc, n_lat=n_lat, n_ctx=n_ctx, n_all=n_all, tm_mm=tm_mm, tm_tok=tm_tok,
        group_map=lambda tm: (lambda i: jnp.where(i < n_lat // tm, (i * tm) // L, bs)),
        pos_map=lambda tm: (lambda i: jnp.where(i < n_lat // tm, i % (L // tm), L // tm)),
    )

    n_mod_rows = -(-(bs + 1) // 8) * 8
    streams = []
    for s in range(n_streams):
        sl = slice(s * bs, (s + 1) * bs)
        streams.append(dict(
            xa=jnp.concatenate([x[sl].reshape(n_lat, D), ctx[sl].reshape(n_ctx, D)], axis=0),
            cvec=jnp.concatenate([c[sl], c_ctx[None, :], jnp.zeros((n_mod_rows - bs - 1, D), F32)], axis=0),
            ffn=None, mod_prev=None))

    head_a = [True] * WA_HEADS + [True] * WA_KV_HEADS + [False] * WA_KV_HEADS
    head_c = [True] * (2 * BRANCH_W // HEAD_DIM) + [False] * (BRANCH_W // HEAD_DIM)
    cos_a, sin_a = _rope_tables(L, tm_mm, head_a)
    cos_c, sin_c = _rope_tables(L, tm_mm, head_c)

    for i in range(depth):
        ctx_out = i < depth - 1
        n_tok = n_all if ctx_out else n_lat
        lam_init = 0.8 - 0.6 * math.exp(-0.3 * i)
        w_in_b = w_in[i].astype(BF16)
        w_a, w_c, w_bd = (w_in_b[:, :A_COLS], w_in_b[:, A_COLS:A_COLS + C_COLS], w_in_b[:, A_COLS + C_COLS:])
        w_g = jnp.transpose(w_gate[i], (1, 0, 2)).reshape(D, N_BRANCH * D).astype(BF16)
        b_g = b_gate[i].reshape(1, N_BRANCH * D)
        w_br, w_o = w_branch[i].astype(BF16), w_out[i].astype(BF16)
        w_q, k1, k2 = peer_wq[i].astype(BF16), peer_k1[i].astype(BF16), peer_k2[i].astype(BF16)
        pool_w_b = pool_w[i].astype(BF16)
        u_packed = _pack_bf16_pairs(lax.bitcast_convert_type(peer_u[i], jnp.int32))
        v_packed = _pack_bf16_pairs(lax.bitcast_convert_type(peer_v[i], jnp.int32))
        lam_params = jnp.stack([lam_q1[i], lam_k1[i], lam_q2[i], lam_k2[i]], axis=0)

        for st in streams:
            mod = mod_vectors(st["cvec"], w_mod[i], b_mod[i]).reshape(n_mod_rows, 6, D)
            if st["ffn"] is None:
                xa = st["xa"]
                (h,) = pre_norm(geo, n_all, xa, None, None, mod, norm1_g[i][None, :])
            else:
                xa, h = pre_norm(geo, n_all, st["xa"], st["ffn"], st["mod_prev"], mod, norm1_g[i][None, :])
            sa = token_matmul(geo, n_all, h, w_a, "rope", (cos_a, sin_a))
            sc = token_matmul(geo, n_all, h, w_c, "rope", (cos_c, sin_c))
            ya = window_attention(geo, sa, attn_sink[i], ctx_out)
            yc = diff_attention(geo, sc, lam_params, diff_norm_g[i][None, :], lam_init, ctx_out)
            bd = token_matmul(geo, n_tok, h, w_bd)
            yb, yd = local_mixers(geo, n_tok, bd, conv_w[i], conv_b[i][None, :], conv_ln_g[i][None, :],
                                  conv_ln_b[i][None, :], pool_w_b, pool_scale[i][None, :])
            gates = token_matmul(geo, n_tok, h, w_g, "sigmoid", (b_g,))
            xa, h2b, h2p = merge_branches(geo, n_tok, ya, yc, yb, yd, gates, xa, mod, norm2_g[i][None, :],
                                          w_br, w_o)
            idx, gate = peer_route(geo, n_tok, h2b, w_q, k1, k2)
            dots = sc_peer_dots(h2p, idx, u_packed, n_tok)
            wgt = peer_activation(geo, n_tok, gate, dots)
            st.update(xa=xa, ffn=sc_peer_wsum(wgt, idx, v_packed, n_tok), mod_prev=mod)

    outs = [pre_norm(geo, n_lat, st["xa"], st["ffn"], st["mod_prev"], st["mod_prev"], final_g[None, :],
                     final=True)[0].reshape(bs, L, D) for st in streams]
    return jnp.concatenate(outs, axis=0)
```
